```python
import math
import jax, jax.numpy as jnp
from jax import lax
import numpy as np

D_MODEL = 2048
BATCH = 1
SEQ = 16384
DEPTH = 2

CHUNK = 64
Q_BLOCK = 128
HEAD_DIM = 128
N_HEADS_A = 6
N_HEADS_B = 6
N_HEADS_C = 4
WIDTH_A = N_HEADS_A * HEAD_DIM
WIDTH_B = N_HEADS_B * HEAD_DIM
WIDTH_C = N_HEADS_C * HEAD_DIM
DIFF_DIM = HEAD_DIM // 2
LEFT_CHUNKS = 8
BAND_CHUNKS = LEFT_CHUNKS + 1
REL_CLIP = 256
ROPE_THETA = 500000.0
ROPE_DIM = DIFF_DIM // 4
D_FF = 7168
N_EXPERTS = 8
TOP_K = 2
N_DENSE = (DEPTH + 1) // 2
N_MOE = DEPTH // 2
EPS = 1e-6
NEG_INF = -1e30
IN_WIDTHS = (WIDTH_A, WIDTH_A, WIDTH_A, N_HEADS_A,
             WIDTH_B, WIDTH_B, WIDTH_B,
             WIDTH_C, WIDTH_C, WIDTH_C)
D_IN = 3 * (WIDTH_A + WIDTH_B + WIDTH_C) + N_HEADS_A

kernel_name = "hybrid_fox_chunkband_diffattn_sandwich_adaln_moe"


def rms_norm(x, g):
    xf = x.astype(jnp.float32)
    y = xf * lax.rsqrt(jnp.mean(xf * xf, axis=-1, keepdims=True) + EPS)
    return (y * g.astype(jnp.float32)).astype(x.dtype)


def partial_rope(x, positions):
    half = ROPE_DIM // 2
    inv = ROPE_THETA ** (-(jnp.arange(half, dtype=jnp.float32) * 2.0 / ROPE_DIM))
    ang = positions.astype(jnp.float32)[..., None] * inv
    cos = jnp.cos(ang)[:, :, None, None, :]
    sin = jnp.sin(ang)[:, :, None, None, :]
    xr = x[..., :ROPE_DIM].astype(jnp.float32)
    x1, x2 = xr[..., :half], xr[..., half:]
    rot = jnp.concatenate([x1 * cos - x2 * sin, x2 * cos + x1 * sin], axis=-1)
    return jnp.concatenate([rot.astype(x.dtype), x[..., ROPE_DIM:]], axis=-1)


def forgetting_attention(q, k, v, log_f):
    B, S, H, d = q.shape
    nb = S // Q_BLOCK
    F = jnp.cumsum(log_f, axis=1)
    Ft = F.transpose(0, 2, 1)
    qb = q.reshape(B, nb, Q_BLOCK, H, d).transpose(1, 0, 2, 3, 4)
    Fb = F.reshape(B, nb, Q_BLOCK, H).transpose(1, 0, 3, 2)
    kpos = jnp.arange(S)
    scale = d ** -0.5

    def block(args):
        qi, Fi, i = args
        s = jnp.einsum('bqhd,bkhd->bhqk', qi, k, preferred_element_type=jnp.float32) * scale
        s = s + Fi[..., None] - Ft[:, :, None, :]
        qpos = i * Q_BLOCK + jnp.arange(Q_BLOCK)
        s = jnp.where(kpos[None, :] <= qpos[:, None], s, NEG_INF)
        p = jax.nn.softmax(s, axis=-1)
        return jnp.einsum('bhqk,bkhd->bqhd', p.astype(v.dtype), v)

    out = lax.map(block, (qb, Fb, jnp.arange(nb)))
    return out.transpose(1, 0, 2, 3, 4).reshape(B, S, H, d)


def chunk_band_attention(q, k, v, rel_bias):
    B, S, H, d = q.shape
    nc = S // CHUNK
    band = BAND_CHUNKS * CHUNK
    pad = ((0, 0), (LEFT_CHUNKS * CHUNK, 0), (0, 0), (0, 0))
    kp = jnp.pad(k, pad).reshape(B, nc + LEFT_CHUNKS, CHUNK, H, d)
    vp = jnp.pad(v, pad).reshape(B, nc + LEFT_CHUNKS, CHUNK, H, d)
    k_band = jnp.concatenate([kp[:, j:j + nc] for j in range(BAND_CHUNKS)], axis=2)
    v_band = jnp.concatenate([vp[:, j:j + nc] for j in range(BAND_CHUNKS)], axis=2)
    qc = q.reshape(B, nc, CHUNK, H, d)
    s = jnp.einsum('bnqhd,bnkhd->bnhqk', qc, k_band, preferred_element_type=jnp.float32) * (d ** -0.5)
    s_rel = jnp.arange(band) - LEFT_CHUNKS * CHUNK
    dist = jnp.arange(CHUNK)[:, None] - s_rel[None, :]
    idx = jnp.clip(dist, -REL_CLIP, REL_CLIP) + REL_CLIP
    s = s + rel_bias[:, idx].astype(jnp.float32)
    valid = (jnp.arange(nc)[:, None] * CHUNK + s_rel[None, :]) >= 0
    s = jnp.where(valid[None, :, None, None, :], s, NEG_INF)
    p = jax.nn.softmax(s, axis=-1)
    o = jnp.einsum('bnhqk,bnkhd->bnqhd', p.astype(v.dtype), v_band)
    return o.reshape(B, S, H, d)


def diff_attention(q, k, v, lam):
    B, S, H, _, dd = q.shape
    nb = S // Q_BLOCK
    qb = q.reshape(B, nb, Q_BLOCK, H, 2, dd).transpose(1, 0, 2, 3, 4, 5)
    kchunk = jnp.arange(S) // CHUNK
    scale = dd ** -0.5

    def block(args):
        qi, i = args
        s = jnp.einsum('bqhnd,bkhnd->bnhqk', qi, k, preferred_element_type=jnp.float32) * scale
        qchunk = (i * Q_BLOCK + jnp.arange(Q_BLOCK)) // CHUNK
        s = jnp.where(kchunk[None, :] <= qchunk[:, None], s, NEG_INF)
        p = jax.nn.softmax(s, axis=-1)
        w = p[:, 0] - lam * p[:, 1]
        return jnp.einsum('bhqk,bkhd->bqhd', w.astype(v.dtype), v)

    out = lax.map(block, (qb, jnp.arange(nb)))
    return out.transpose(1, 0, 2, 3, 4).reshape(B, S, H, v.shape[-1])


def hybrid_mixer(h, positions, w_in, b_f, rel_bias, lam, onorm, w_o, layer_idx):
    B, S, _ = h.shape
    proj = jnp.einsum('bsd,de->bse', h, w_in)
    split_pts = np.cumsum(np.array(IN_WIDTHS))[:-1].tolist()
    qa, ka, va, fa, qb, kb, vb, qc, kc, vc = jnp.split(proj, split_pts, axis=-1)
    heads = lambda t, n: t.reshape(B, S, n, HEAD_DIM)
    log_f = jax.nn.log_sigmoid(fa.astype(jnp.float32) + b_f.astype(jnp.float32))
    o_a = forgetting_attention(heads(qa, N_HEADS_A), heads(ka, N_HEADS_A), heads(va, N_HEADS_A), log_f)
    o_b = chunk_band_attention(heads(qb, N_HEADS_B), heads(kb, N_HEADS_B), heads(vb, N_HEADS_B), rel_bias)
    qc = partial_rope(qc.reshape(B, S, N_HEADS_C, 2, DIFF_DIM), positions)
    kc = partial_rope(kc.reshape(B, S, N_HEADS_C, 2, DIFF_DIM), positions)
    lam_init = 0.8 - 0.6 * math.exp(-0.3 * layer_idx)
    lf = lam.astype(jnp.float32)
    lam_full = jnp.exp(jnp.sum(lf[0] * lf[1])) - jnp.exp(jnp.sum(lf[2] * lf[3])) + lam_init
    o_c = diff_attention(qc, kc, heads(vc, N_HEADS_C), lam_full)
    o = jnp.concatenate([rms_norm(o_a, onorm[0]),
                         rms_norm(o_b, onorm[1]),
                         rms_norm(o_c, onorm[2]) * (1.0 - lam_init)], axis=2)
    return jnp.einsum('bse,ed->bsd', o.reshape(B, S, -1), w_o)


def swiglu(h, wg, wu, wd):
    a = jnp.einsum('bsd,df->bsf', h, wg)
    u = jnp.einsum('bsd,df->bsf', h, wu)
    return jnp.einsum('bsf,fd->bsd', jax.nn.silu(a) * u, wd)


def moe_ffn(h, router_w, router_b, wg, wu, wd):
    logits = jnp.einsum('bsd,de->bse', h, router_w).astype(jnp.float32) + router_b.astype(jnp.float32)
    top_v, top_i = lax.top_k(logits, TOP_K)
    gates = jax.nn.softmax(top_v, axis=-1)
    combine = jnp.sum(jax.nn.one_hot(top_i, N_EXPERTS, dtype=jnp.float32) * gates[..., None], axis=-2)
    out = jnp.zeros(h.shape, jnp.float32)
    for e in range(N_EXPERTS):
        out = out + combine[..., e:e + 1] * swiglu(h, wg[e], wu[e], wd[e]).astype(jnp.float32)
    return out.astype(h.dtype)


def setup_inputs(seed: int = 0) -> dict:
    key = jax.random.key(seed)
    ks = jax.random.split(key, 21)
    f32 = jnp.float32
    D = D_MODEL
    nrm = lambda k, shape, scale: jax.random.normal(k, shape, f32) * scale
    x = nrm(ks[0], (BATCH, SEQ, D), 1.0)
    c = nrm(ks[1], (BATCH, D), 1.0)
    offset = jax.random.randint(ks[2], (BATCH, 1), 0, 1024, dtype=jnp.int32) * CHUNK
    positions = offset + jnp.arange(SEQ, dtype=jnp.int32)[None, :]
    return {
        "x": x,
        "c": c,
        "positions": positions,
        "mod_w": nrm(ks[3], (DEPTH, D, 6 * D), 0.5 * D ** -0.5),
        "mod_b": nrm(ks[4], (DEPTH, 6 * D), 0.02),
        "norm_g": 1.0 + nrm(ks[5], (DEPTH, 4, D), 0.02),
        "w_in": nrm(ks[6], (DEPTH, D, D_IN), D ** -0.5),
        "b_f": nrm(ks[7], (DEPTH, N_HEADS_A), 0.5),
        "rel_bias": nrm(ks[8], (DEPTH, N_HEADS_B, 2 * REL_CLIP + 1), 0.5),
        "lam": nrm(ks[9], (DEPTH, 4, DIFF_DIM), 0.1),
        "onorm": 1.0 + nrm(ks[10], (DEPTH, 3, HEAD_DIM), 0.02),
        "w_o": nrm(ks[11], (DEPTH, D, D), D ** -0.5),
        "ffn_wg": nrm(ks[12], (N_DENSE, D, D_FF), D ** -0.5),
        "ffn_wu": nrm(ks[13], (N_DENSE, D, D_FF), D ** -0.5),
        "ffn_wd": nrm(ks[14], (N_DENSE, D_FF, D), D_FF ** -0.5),
        "router_w": nrm(ks[15], (N_MOE, D, N_EXPERTS), D ** -0.5),
        "router_b": nrm(ks[16], (N_MOE, N_EXPERTS), 0.01),
        "exp_wg": nrm(ks[17], (N_MOE, N_EXPERTS, D, D_FF), D ** -0.5),
        "exp_wu": nrm(ks[18], (N_MOE, N_EXPERTS, D, D_FF), D ** -0.5),
        "exp_wd": nrm(ks[19], (N_MOE, N_EXPERTS, D_FF, D), D_FF ** -0.5),
    }


def reference(x, c, positions, mod_w, mod_b, norm_g, w_in, b_f, rel_bias, lam, onorm, w_o,
              ffn_wg, ffn_wu, ffn_wd, router_w, router_b, exp_wg, exp_wu, exp_wd):
    cond = jax.nn.silu(c)
    for l in range(DEPTH):
        mod = (jnp.einsum('bd,de->be', cond, mod_w[l]) + mod_b[l])[:, None, :]
        sh_m, sc_m, g_m, sh_f, sc_f, g_f = jnp.split(mod, 6, axis=-1)
        h = rms_norm(x, norm_g[l, 0]) * (1.0 + sc_m) + sh_m
        y = hybrid_mixer(h, positions, w_in[l], b_f[l], rel_bias[l], lam[l], onorm[l], w_o[l], l)
        x = x + g_m * rms_norm(y, norm_g[l, 1])
        h = rms_norm(x, norm_g[l, 2]) * (1.0 + sc_f) + sh_f
        i = l // 2
        if l % 2 == 0:
            y = swiglu(h, ffn_wg[i], ffn_wu[i], ffn_wd[i])
        else:
            y = moe_ffn(h, router_w[i], router_b[i], exp_wg[i], exp_wu[i], exp_wd[i])
        x = x + g_f * rms_norm(y, norm_g[l, 3])
    return x
```

```python
import functools
import math

import numpy as np
import jax
import jax.numpy as jnp
from jax import lax
from jax.experimental import pallas as pl
from jax.experimental.pallas import tpu as pltpu

F32 = jnp.float32
BF16 = jnp.bfloat16

CHUNK = 64
HEAD_DIM = 128
N_HEADS_A = 6
N_HEADS_B = 6
N_HEADS_C = 4
WIDTH_A = N_HEADS_A * HEAD_DIM
WIDTH_B = N_HEADS_B * HEAD_DIM
WIDTH_C = N_HEADS_C * HEAD_DIM
DIFF_DIM = HEAD_DIM // 2
LEFT_CHUNKS = 8
REL_CLIP = 256
ROPE_THETA = 500000.0
ROPE_DIM = DIFF_DIM // 4
N_EXPERTS = 8
TOP_K = 2
EPS = 1e-6
NEG_INF = -1e30

LANES = 128
VMEM_LIMIT = 56 * 1024 * 1024


def _cparams(sem):
    return pltpu.CompilerParams(dimension_semantics=sem, vmem_limit_bytes=VMEM_LIMIT)


def _tile(n, pref):
    t = min(n, pref)
    while n % t:
        t //= 2
    return t


def _rms(x, g):
    return (x * lax.rsqrt(jnp.mean(x * x, axis=-1, keepdims=True) + EPS)) * g


def _mods_kernel(c_ref, w_ref, b_ref, o_ref):
    c = c_ref[...]
    cond = c * jax.nn.sigmoid(c)
    o_ref[...] = jnp.sum(w_ref[...] * cond, axis=0, keepdims=True) + b_ref[...]


def _mods(c, mod_w, mod_b):
    depth, d, n = mod_w.shape
    tn = _tile(n, 1024)
    return pl.pallas_call(
        _mods_kernel,
        grid=(depth, n // tn),
        in_specs=[
            pl.BlockSpec((d, 1), lambda l, j: (0, 0)),
            pl.BlockSpec((None, d, tn), lambda l, j: (l, 0, j)),
            pl.BlockSpec((None, 1, tn), lambda l, j: (l, 0, j)),
        ],
        out_specs=pl.BlockSpec((None, 1, tn), lambda l, j: (l, 0, j)),
        out_shape=jax.ShapeDtypeStruct((depth, 1, n), F32),
        compiler_params=_cparams(("arbitrary", "arbitrary")),
        name="adaln_mods",
    )(c.reshape(d, 1), mod_w, mod_b.reshape(depth, 1, n))


ROPE_COL_LO = 3 * WIDTH_A + 3 * WIDTH_B
ROPE_COL_HI = ROPE_COL_LO + 2 * WIDTH_C
INPROJ_TN = 512


def _inproj_kernel(x_ref, g_ref, sc_ref, sh_ref, w_ref, cs_ref, wf_ref, bf_ref, pos_ref,
                   inv_ref, m1_ref, m2_ref, o_ref, lf_ref, h_scr, cos_scr, sin_scr):
    j = pl.program_id(1)

    @pl.when(j == 0)
    def _():
        h = _rms(x_ref[...], g_ref[...]) * (1.0 + sc_ref[...]) + sh_ref[...]
        hb = h.astype(BF16)
        h_scr[...] = hb
        fa = jnp.dot(hb, wf_ref[...], preferred_element_type=F32) + bf_ref[...]
        lf_ref[...] = jax.nn.log_sigmoid(fa)
        ang = pos_ref[...] * inv_ref[...]
        cos_scr[...] = jnp.cos(ang)
        sin_scr[...] = jnp.sin(ang)

    y = jnp.dot(h_scr[...], w_ref[...], preferred_element_type=F32) * cs_ref[...]
    is_rope = jnp.logical_and(j >= ROPE_COL_LO // INPROJ_TN, j < ROPE_COL_HI // INPROJ_TN)

    @pl.when(is_rope)
    def _():
        c = cos_scr[...]
        s = sin_scr[...]
        m1 = m1_ref[...]
        m2 = m2_ref[...]
        for hh in range(INPROJ_TN // LANES):
            yh = y[:, hh * LANES:(hh + 1) * LANES]
            up = pltpu.roll(yh, LANES - ROPE_DIM // 2, 1)
            dn = pltpu.roll(yh, ROPE_DIM // 2, 1)
            o_ref[:, hh * LANES:(hh + 1) * LANES] = (yh * c + s * (m1 * up + m2 * dn)).astype(BF16)

    @pl.when(jnp.logical_not(is_rope))
    def _():
        o_ref[...] = y.astype(BF16)


def _rope_lane_tables():
    lane = np.arange(LANES)
    r = lane % DIFF_DIM
    half = ROPE_DIM // 2
    inv = np.where(r < ROPE_DIM, ROPE_THETA ** (-((r % half) * 2.0 / ROPE_DIM)), 0.0)
    m1 = np.where(r < half, -1.0, 0.0)
    m2 = np.where((r >= half) & (r < ROPE_DIM), 1.0, 0.0)
    f = lambda a: jnp.asarray(a.reshape(1, LANES), F32)
    return f(inv), f(m1), f(m2)


def _inproj(x, g, sc, sh, w_main, colscale, w_f, b_f, pos_col):
    s, d = x.shape
    n = w_main.shape[1]
    tm = _tile(s, 1024)
    tn = INPROJ_TN
    inv, m1, m2 = _rope_lane_tables()
    row = lambda i, j: (i, 0)
    const = lambda i, j: (0, 0)
    return pl.pallas_call(
        _inproj_kernel,
        grid=(s // tm, n // tn),
        in_specs=[
            pl.BlockSpec((tm, d), row),
            pl.BlockSpec((1, d), const),
            pl.BlockSpec((1, d), const),
            pl.BlockSpec((1, d), const),
            pl.BlockSpec((d, tn), lambda i, j: (0, j)),
            pl.BlockSpec((1, tn), lambda i, j: (0, j)),
            pl.BlockSpec((d, LANES), const),
            pl.BlockSpec((1, LANES), const),
            pl.BlockSpec((tm, 1), row),
            pl.BlockSpec((1, LANES), const),
            pl.BlockSpec((1, LANES), const),
            pl.BlockSpec((1, LANES), const),
        ],
        out_specs=[
            pl.BlockSpec((tm, tn), lambda i, j: (i, j)),
            pl.BlockSpec((tm, LANES), row),
        ],
        out_shape=[
            jax.ShapeDtypeStruct((s, n), BF16),
            jax.ShapeDtypeStruct((s, LANES), F32),
        ],
        scratch_shapes=[
            pltpu.VMEM((tm, d), BF16),
            pltpu.VMEM((tm, LANES), F32),
            pltpu.VMEM((tm, LANES), F32),
        ],
        compiler_params=_cparams(("arbitrary", "arbitrary")),
        name="inproj",
    )(x, g, sc, sh, w_main, colscale, w_f, b_f, pos_col, inv, m1, m2)


def _cumsum_kernel(x_ref, o_ref, ot_ref, carry):
    @pl.when(pl.program_id(0) == 0)
    def _():
        carry[...] = jnp.zeros_like(carry)

    x = x_ref[...]
    tb = x.shape[0]
    r = lax.broadcasted_iota(jnp.int32, (tb, tb), 0)
    c = lax.broadcasted_iota(jnp.int32, (tb, tb), 1)
    tri = jnp.where(c <= r, 1.0, 0.0).astype(BF16)
    x1 = x.astype(BF16)
    r1 = x - x1.astype(F32)
    x2 = r1.astype(BF16)
    x3 = (r1 - x2.astype(F32)).astype(BF16)
    y = (jnp.dot(tri, x1, preferred_element_type=F32)
         + jnp.dot(tri, x2, preferred_element_type=F32)
         + jnp.dot(tri, x3, preferred_element_type=F32)) + carry[...]
    o_ref[...] = y
    ot_ref[...] = y.T[:8, :]
    carry[...] = y[tb - 1:tb, :]


def _cumsum_rows(x):
    s = x.shape[0]
    tb = _tile(s, 256)
    return pl.pallas_call(
        _cumsum_kernel,
        grid=(s // tb,),
        in_specs=[pl.BlockSpec((tb, LANES), lambda i: (i, 0))],
        out_specs=[pl.BlockSpec((tb, LANES), lambda i: (i, 0)),
                   pl.BlockSpec((8, tb), lambda i: (0, i))],
        out_shape=[jax.ShapeDtypeStruct((s, LANES), F32),
                   jax.ShapeDtypeStruct((8, s), F32)],
        scratch_shapes=[pltpu.VMEM((1, LANES), F32)],
        compiler_params=_cparams(("arbitrary",)),
        name="cumsum_rows",
    )(x)


def _qk(q, k):
    return lax.dot_general(q, k, (((1,), (1,)), ((), ())), preferred_element_type=F32)


def _online_update(s, v, m, l, acc):
    m_new = jnp.maximum(m, jnp.max(s, axis=-1, keepdims=True))
    p = jnp.exp(s - m_new)
    alpha = jnp.exp(m - m_new)
    l = alpha * l + jnp.sum(p, axis=-1, keepdims=True)
    acc = alpha * acc + jnp.dot(p.astype(BF16), v, preferred_element_type=F32)
    return m_new, l, acc


def _attn_a_kernel(q_ref, k_ref, v_ref, fc_ref, fr_ref, g_ref, o_ref, *, tq):
    h = pl.program_id(0)
    qi = pl.program_id(1)
    q = q_ref[...]
    lane = lax.broadcasted_iota(jnp.int32, (1, LANES), 1)
    fref = jnp.sum(jnp.where(lane == h, fc_ref[0:1, :], 0.0), axis=-1, keepdims=True)

    def block(kb, carry, masked):
        m, l, acc = carry
        start = pl.multiple_of(kb * tq, tq)
        k = k_ref[pl.ds(start, tq), :]
        v = v_ref[pl.ds(start, tq), :]
        fk = fr_ref[:, pl.ds(start, tq)]
        s = _qk(q, k) + (fref - fk)
        if masked:
            r = lax.broadcasted_iota(jnp.int32, (tq, tq), 0)
            c = lax.broadcasted_iota(jnp.int32, (tq, tq), 1)
            s = jnp.where(c <= r, s, NEG_INF)
        return _online_update(s, v, m, l, acc)

    init = (jnp.full((tq, 1), NEG_INF, F32), jnp.zeros((tq, 1), F32), jnp.zeros((tq, HEAD_DIM), F32))
    carry = lax.fori_loop(0, qi, lambda kb, cr: block(kb, cr, False), init)
    m, l, acc = block(qi, carry, True)
    o_ref[...] = _rms(acc / l, g_ref[...]).astype(BF16)


def _attn_a(proj, fcol, frow, g):
    s = proj.shape[0]
    tq = _tile(s, 512)
    nh = N_HEADS_A
    kern = functools.partial(_attn_a_kernel, tq=tq)
    return pl.pallas_call(
        kern,
        grid=(nh, s // tq),
        in_specs=[
            pl.BlockSpec((tq, HEAD_DIM), lambda h, i: (i, h)),
            pl.BlockSpec((s, HEAD_DIM), lambda h, i: (0, nh + h)),
            pl.BlockSpec((s, HEAD_DIM), lambda h, i: (0, 2 * nh + h)),
            pl.BlockSpec((tq, LANES), lambda h, i: (i, 0)),
            pl.BlockSpec((None, 1, s), lambda h, i: (h, 0, 0)),
            pl.BlockSpec((1, HEAD_DIM), lambda h, i: (0, 0)),
        ],
        out_specs=pl.BlockSpec((tq, HEAD_DIM), lambda h, i: (i, h)),
        out_shape=jax.ShapeDtypeStruct((s, WIDTH_A), BF16),
        compiler_params=_cparams(("arbitrary", "arbitrary")),
        name="attn_forget",
    )(proj, proj, proj, fcol, frow, g)


TQB = 256
BAND = LEFT_CHUNKS * CHUNK
NWIN = (BAND + TQB) // TQB


def _attn_b_kernel(q_ref, k_ref, v_ref, bias_ref, g_ref, o_ref):
    qi = pl.program_id(1)
    q = q_ref[...]
    ss = []
    for w in range(NWIN):
        kb = qi - (NWIN - 1) + w
        start = pl.multiple_of(jnp.maximum(kb, 0) * TQB, TQB)
        sw = _qk(q, k_ref[pl.ds(start, TQB), :]) + bias_ref[:, w * TQB:(w + 1) * TQB]
        ss.append(jnp.where(kb >= 0, sw, NEG_INF))
    m = functools.reduce(jnp.maximum, [jnp.max(sw, axis=-1, keepdims=True) for sw in ss])
    l = jnp.zeros((TQB, 1), F32)
    acc = jnp.zeros((TQB, HEAD_DIM), F32)
    for w in range(NWIN):
        kb = qi - (NWIN - 1) + w
        start = pl.multiple_of(jnp.maximum(kb, 0) * TQB, TQB)
        p = jnp.exp(ss[w] - m)
        l = l + jnp.sum(p, axis=-1, keepdims=True)
        acc = acc + jnp.dot(p.astype(BF16), v_ref[pl.ds(start, TQB), :], preferred_element_type=F32)
    o_ref[...] = _rms(acc / l, g_ref[...]).astype(BF16)


def _band_bias_table(rel_bias):
    qpos = np.arange(TQB)[:, None]
    kpos = np.arange(NWIN * TQB)[None, :] - (NWIN - 1) * TQB
    idx = np.clip(qpos - kpos, -REL_CLIP, REL_CLIP) + REL_CLIP
    qc = qpos // CHUNK
    kc = np.floor_divide(kpos, CHUNK)
    valid = (kc <= qc) & (kc >= qc - LEFT_CHUNKS)
    return jnp.where(jnp.asarray(valid)[None], rel_bias.astype(F32)[:, jnp.asarray(idx)], NEG_INF)


def _attn_b(proj, bias_tbl, g):
    s = proj.shape[0]
    nh = N_HEADS_B
    c0 = 3 * N_HEADS_A
    return pl.pallas_call(
        _attn_b_kernel,
        grid=(nh, s // TQB),
        in_specs=[
            pl.BlockSpec((TQB, HEAD_DIM), lambda h, i: (i, c0 + h)),
            pl.BlockSpec((s, HEAD_DIM), lambda h, i: (0, c0 + nh + h)),
            pl.BlockSpec((s, HEAD_DIM), lambda h, i: (0, c0 + 2 * nh + h)),
            pl.BlockSpec((None, TQB, NWIN * TQB), lambda h, i: (h, 0, 0)),
            pl.BlockSpec((1, HEAD_DIM), lambda h, i: (0, 0)),
        ],
        out_specs=pl.BlockSpec((TQB, HEAD_DIM), lambda h, i: (i, h)),
        out_shape=jax.ShapeDtypeStruct((s, WIDTH_B), BF16),
        compiler_params=_cparams(("arbitrary", "arbitrary")),
        name="attn_band",
    )(proj, proj, proj, bias_tbl, g)


def _attn_c_kernel(q_ref, k_ref, v_ref, lam_ref, g_ref, o_ref, *, tq, lam_init):
    qi = pl.program_id(1)
    q = q_ref[...]
    lane = lax.broadcasted_iota(jnp.int32, (1, HEAD_DIM), 1)
    zero = jnp.zeros_like(q)
    q0 = jnp.where(lane < DIFF_DIM, q, zero)
    q1 = jnp.where(lane >= DIFF_DIM, q, zero)

    def block(kb, carry, masked):
        m0, l0, a0, m1, l1, a1 = carry
        start = pl.multiple_of(kb * tq, tq)
        k = k_ref[pl.ds(start, tq), :]
        v = v_ref[pl.ds(start, tq), :]
        s0 = _qk(q0, k)
        s1 = _qk(q1, k)
        if masked:
            r = lax.broadcasted_iota(jnp.int32, (tq, tq), 0) // CHUNK
            c = lax.broadcasted_iota(jnp.int32, (tq, tq), 1) // CHUNK
            ok = c <= r
            s0 = jnp.where(ok, s0, NEG_INF)
            s1 = jnp.where(ok, s1, NEG_INF)
        m0, l0, a0 = _online_update(s0, v, m0, l0, a0)
        m1, l1, a1 = _online_update(s1, v, m1, l1, a1)
        return m0, l0, a0, m1, l1, a1

    st = (jnp.full((tq, 1), NEG_INF, F32), jnp.zeros((tq, 1), F32), jnp.zeros((tq, HEAD_DIM), F32))
    carry = lax.fori_loop(0, qi, lambda kb, cr: block(kb, cr, False), st + st)
    m0, l0, a0, m1, l1, a1 = block(qi, carry, True)
    lam = lam_ref[...]
    lam_full = (jnp.exp(jnp.sum(lam[0:1] * lam[1:2], axis=-1, keepdims=True))
                - jnp.exp(jnp.sum(lam[2:3] * lam[3:4], axis=-1, keepdims=True)) + lam_init)
    o = a0 / l0 - lam_full * (a1 / l1)
    o_ref[...] = (_rms(o, g_ref[...]) * (1.0 - lam_init)).astype(BF16)


def _attn_c(proj, lam, g, lam_init):
    s = proj.shape[0]
    tq = _tile(s, 512)
    nh = N_HEADS_C
    c0 = 3 * N_HEADS_A + 3 * N_HEADS_B
    kern = functools.partial(_attn_c_kernel, tq=tq, lam_init=lam_init)
    return pl.pallas_call(
        kern,
        grid=(nh, s // tq),
        in_specs=[
            pl.BlockSpec((tq, HEAD_DIM), lambda h, i: (i, c0 + h)),
            pl.BlockSpec((s, HEAD_DIM), lambda h, i: (0, c0 + nh + h)),
            pl.BlockSpec((s, HEAD_DIM), lambda h, i: (0, c0 + 2 * nh + h)),
            pl.BlockSpec((4, DIFF_DIM), lambda h, i: (0, 0)),
            pl.BlockSpec((1, HEAD_DIM), lambda h, i: (0, 0)),
        ],
        out_specs=pl.BlockSpec((tq, HEAD_DIM), lambda h, i: (i, h)),
        out_shape=jax.ShapeDtypeStruct((s, WIDTH_C), BF16),
        compiler_params=_cparams(("arbitrary", "arbitrary")),
        name="attn_diff",
    )(proj, proj, proj, lam, g)


def _residual(x, y, g_post, gate):
    return x + gate * _rms(y, g_post)


def _outproj_kernel(oa_ref, ob_ref, oc_ref, wa_ref, wb_ref, wc_ref, x_ref, gp_ref, gate_ref,
                    gn_ref, sc_ref, sh_ref, xo_ref, ho_ref):
    y = (jnp.dot(oa_ref[...], wa_ref[...], preferred_element_type=F32)
         + jnp.dot(ob_ref[...], wb_ref[...], preferred_element_type=F32)
         + jnp.dot(oc_ref[...], wc_ref[...], preferred_element_type=F32))
    xn = _residual(x_ref[...], y, gp_ref[...], gate_ref[...])
    xo_ref[...] = xn
    ho_ref[...] = (_rms(xn, gn_ref[...]) * (1.0 + sc_ref[...]) + sh_ref[...]).astype(ho_ref.dtype)


def _outproj(oa, ob, oc, w_o, x, g_post, gate, g_next, sc, sh, h_dtype):
    s, d = x.shape
    tm = _tile(s, 256)
    row = lambda i: (i, 0)
    const = lambda i: (0, 0)
    assert WIDTH_A == WIDTH_B and (WIDTH_A + WIDTH_B) % WIDTH_C == 0
    return pl.pallas_call(
        _outproj_kernel,
        grid=(s // tm,),
        in_specs=[
            pl.BlockSpec((tm, WIDTH_A), row),
            pl.BlockSpec((tm, WIDTH_B), row),
            pl.BlockSpec((tm, WIDTH_C), row),
            pl.BlockSpec((WIDTH_A, d), lambda i: (0, 0)),
            pl.BlockSpec((WIDTH_B, d), lambda i: (1, 0)),
            pl.BlockSpec((WIDTH_C, d), lambda i: ((WIDTH_A + WIDTH_B) // WIDTH_C, 0)),
            pl.BlockSpec((tm, d), row),
            pl.BlockSpec((1, d), const),
            pl.BlockSpec((1, d), const),
            pl.BlockSpec((1, d), const),
            pl.BlockSpec((1, d), const),
            pl.BlockSpec((1, d), const),
        ],
        out_specs=[pl.BlockSpec((tm, d), row), pl.BlockSpec((tm, d), row)],
        out_shape=[jax.ShapeDtypeStruct((s, d), F32), jax.ShapeDtypeStruct((s, d), h_dtype)],
        compiler_params=_cparams(("arbitrary",)),
        name="outproj_residual",
    )(oa, ob, oc, w_o, w_o, w_o, x, g_post, gate, g_next, sc, sh)


def _gateup_kernel(h_ref, wg_ref, wu_ref, o_ref):
    h = h_ref[...]
    a = jnp.dot(h, wg_ref[...], preferred_element_type=F32)
    u = jnp.dot(h, wu_ref[...], preferred_element_type=F32)
    o_ref[...] = ((a * jax.nn.sigmoid(a)) * u).astype(BF16)


def _gateup(h, wg, wu):
    s, d = h.shape
    f = wg.shape[1]
    tm = _tile(s, 1024)
    tf = _tile(f, 512)
    return pl.pallas_call(
        _gateup_kernel,
        grid=(s // tm, f // tf),
        in_specs=[
            pl.BlockSpec((tm, d), lambda i, j: (i, 0)),
            pl.BlockSpec((d, tf), lambda i, j: (0, j)),
            pl.BlockSpec((d, tf), lambda i, j: (0, j)),
        ],
        out_specs=pl.BlockSpec((tm, tf), lambda i, j: (i, j)),
        out_shape=jax.ShapeDtypeStruct((s, f), BF16),
        compiler_params=_cparams(("arbitrary", "arbitrary")),
        name="ffn_gateup",
    )(h, wg, wu)


def _down_kernel(a_ref, w_ref, x_ref, gp_ref, gate_ref, o_ref, acc):
    k = pl.program_id(1)

    @pl.when(k == 0)
    def _():
        acc[...] = jnp.zeros_like(acc)

    acc[...] += jnp.dot(a_ref[...], w_ref[...], preferred_element_type=F32)

    @pl.when(k == pl.num_programs(1) - 1)
    def _():
        o_ref[...] = _residual(x_ref[...], acc[...], gp_ref[...], gate_ref[...])


def _down(act, wd, x, g_post, gate):
    s, f = act.shape
    d = wd.shape[1]
    tm = _tile(s, 512)
    tk = _tile(f, 1024)
    return pl.pallas_call(
        _down_kernel,
        grid=(s // tm, f // tk),
        in_specs=[
            pl.BlockSpec((tm, tk), lambda i, k: (i, k)),
            pl.BlockSpec((tk, d), lambda i, k: (k, 0)),
            pl.BlockSpec((tm, d), lambda i, k: (i, 0)),
            pl.BlockSpec((1, d), lambda i, k: (0, 0)),
            pl.BlockSpec((1, d), lambda i, k: (0, 0)),
        ],
        out_specs=pl.BlockSpec((tm, d), lambda i, k: (i, 0)),
        out_shape=jax.ShapeDtypeStruct((s, d), F32),
        scratch_shapes=[pltpu.VMEM((tm, d), F32)],
        compiler_params=_cparams(("arbitrary", "arbitrary")),
        name="ffn_down_residual",
    )(act, wd, x, g_post, gate)


def _split2(a):
    hi = a.astype(BF16)
    return hi, (a - hi.astype(F32)).astype(BF16)


def _router_kernel(h_ref, w_ref, b_ref, route_ref, mh_ref):
    hh, hl = _split2(h_ref[...])
    wh, wl = _split2(w_ref[...])
    logits = (jnp.dot(hh, wh, preferred_element_type=F32)
              + jnp.dot(hh, wl, preferred_element_type=F32)
              + jnp.dot(hl, wh, preferred_element_type=F32)) + b_ref[...]
    lane = lax.broadcasted_iota(jnp.int32, logits.shape, 1)
    logits = jnp.where(lane < N_EXPERTS, logits, NEG_INF)
    v1 = jnp.max(logits, axis=-1, keepdims=True)
    i1 = jnp.min(jnp.where(logits == v1, lane, LANES), axis=-1, keepdims=True)
    rest = jnp.where(lane == i1, NEG_INF, logits)
    v2 = jnp.max(rest, axis=-1, keepdims=True)
    i2 = jnp.min(jnp.where(rest == v2, lane, LANES), axis=-1, keepdims=True)
    e = jnp.exp(v2 - v1)
    g1 = 1.0 / (1.0 + e)
    g2 = e / (1.0 + e)
    route_ref[...] = jnp.where(lane == 0, i1.astype(F32),
                               jnp.where(lane == 1, i2.astype(F32),
                                         jnp.where(lane == 2, g1, jnp.where(lane == 3, g2, 0.0))))
    mh_ref[...] = jnp.where(jnp.logical_or(lane == i1, lane == i2), 1.0, 0.0)


def _router(h, rw_pad, rb_pad):
    s, d = h.shape
    tm = _tile(s, 512)
    return pl.pallas_call(
        _router_kernel,
        grid=(s // tm,),
        in_specs=[
            pl.BlockSpec((tm, d), lambda i: (i, 0)),
            pl.BlockSpec((d, LANES), lambda i: (0, 0)),
            pl.BlockSpec((1, LANES), lambda i: (0, 0)),
        ],
        out_specs=[pl.BlockSpec((tm, LANES), lambda i: (i, 0)),
                   pl.BlockSpec((tm, LANES), lambda i: (i, 0))],
        out_shape=[jax.ShapeDtypeStruct((s, LANES), F32), jax.ShapeDtypeStruct((s, LANES), F32)],
        compiler_params=_cparams(("arbitrary",)),
        name="moe_router",
    )(h, rw_pad, rb_pad)


def _dispatch_kernel(pos_ref, h_ref, init_ref, o_ref, sem):
    del init_ref
    n = h_ref.shape[0]

    def copy(r, k):
        return pltpu.make_async_copy(h_ref.at[pl.ds(r, 1)], o_ref.at[pl.ds(pos_ref[0, 0, 2 * r + k], 1)], sem)

    def issue(r, c):
        copy(r, 0).start()
        copy(r, 1).start()
        return c

    def drain(r, c):
        copy(r, 0).wait()
        copy(r, 1).wait()
        return c

    lax.fori_loop(0, n, issue, 0)
    lax.fori_loop(0, n, drain, 0)


def _dispatch(h, pos, n_slots):
    s, d = h.shape
    tm = _tile(s, 256)
    init = jnp.zeros((n_slots, d), F32)
    return pl.pallas_call(
        _dispatch_kernel,
        grid=(s // tm,),
        in_specs=[
            pl.BlockSpec((1, 1, TOP_K * tm), lambda i: (i, 0, 0), memory_space=pltpu.SMEM),
            pl.BlockSpec((tm, d), lambda i: (i, 0)),
            pl.BlockSpec(memory_space=pl.ANY),
        ],
        out_specs=pl.BlockSpec(memory_space=pl.ANY),
        out_shape=jax.ShapeDtypeStruct((n_slots, d), F32),
        scratch_shapes=[pltpu.SemaphoreType.DMA(())],
        input_output_aliases={2: 0},
        compiler_params=_cparams(("arbitrary",)),
        name="moe_dispatch",
    )(pos.reshape(s // tm, 1, TOP_K * tm), h, init)


def _ggateup_kernel(te_ref, nu_ref, h_ref, wg_ref, wu_ref, o_ref, hb_scr):
    i = pl.program_id(0)
    j = pl.program_id(1)

    @pl.when(i < nu_ref[0])
    def _():
        @pl.when(j == 0)
        def _():
            hb_scr[...] = h_ref[...].astype(BF16)

        h = hb_scr[...]
        a = jnp.dot(h, wg_ref[...], preferred_element_type=F32)
        u = jnp.dot(h, wu_ref[...], preferred_element_type=F32)
        o_ref[...] = ((a * jax.nn.sigmoid(a)) * u).astype(BF16)

    @pl.when(i >= nu_ref[0])
    def _():
        o_ref[...] = jnp.zeros_like(o_ref)


def _ggateup(hs, wg, wu, tile_expert, n_used, tm):
    p, d = hs.shape
    f = wg.shape[2]
    tf = _tile(f, 512)
    return pl.pallas_call(
        _ggateup_kernel,
        grid_spec=pltpu.PrefetchScalarGridSpec(
            num_scalar_prefetch=2,
            grid=(p // tm, f // tf),
            in_specs=[
                pl.BlockSpec((tm, d), lambda i, j, te, nu: (i, 0)),
                pl.BlockSpec((None, d, tf), lambda i, j, te, nu: (te[i], 0, j)),
                pl.BlockSpec((None, d, tf), lambda i, j, te, nu: (te[i], 0, j)),
            ],
            out_specs=pl.BlockSpec((tm, tf), lambda i, j, te, nu: (i, j)),
            scratch_shapes=[pltpu.VMEM((tm, d), BF16)],
        ),
        out_shape=jax.ShapeDtypeStruct((p, f), BF16),
        compiler_params=_cparams(("arbitrary", "arbitrary")),
        name="moe_gateup",
    )(tile_expert, n_used, hs, wg, wu)


def _gdown_kernel(te_ref, nu_ref, a_ref, w_ref, o_ref, acc):
    i = pl.program_id(0)
    k = pl.program_id(1)
    last = k == pl.num_programs(1) - 1

    @pl.when(i < nu_ref[0])
    def _():
        @pl.when(k == 0)
        def _():
            acc[...] = jnp.zeros_like(acc)

        acc[...] += jnp.dot(a_ref[...], w_ref[...], preferred_element_type=F32)

        @pl.when(last)
        def _():
            o_ref[...] = acc[...]

    @pl.when(jnp.logical_and(i >= nu_ref[0], last))
    def _():
        o_ref[...] = jnp.zeros_like(o_ref)


def _gdown(act, wd, tile_expert, n_used, tm):
    p, f = act.shape
    d = wd.shape[2]
    tk = _tile(f, 1024)
    return pl.pallas_call(
        _gdown_kernel,
        grid_spec=pltpu.PrefetchScalarGridSpec(
            num_scalar_prefetch=2,
            grid=(p // tm, f // tk),
            in_specs=[
                pl.BlockSpec((tm, tk), lambda i, k, te, nu: (i, k)),
                pl.BlockSpec((None, tk, d), lambda i, k, te, nu: (te[i], k, 0)),
            ],
            out_specs=pl.BlockSpec((tm, d), lambda i, k, te, nu: (i, 0)),
            scratch_shapes=[pltpu.VMEM((tm, d), F32)],
        ),
        out_shape=jax.ShapeDtypeStruct((p, d), F32),
        compiler_params=_cparams(("arbitrary", "arbitrary")),
        name="moe_down",
    )(tile_expert, n_used, act, wd)


def _combine_kernel(pos_ref, y_ref, route_ref, x_ref, gp_ref, gate_ref, o_ref, buf0, buf1, sem):
    n = x_ref.shape[0]

    def copy(r, k, buf):
        return pltpu.make_async_copy(y_ref.at[pl.ds(pos_ref[0, 0, 2 * r + k], 1)], buf.at[pl.ds(r, 1)], sem)

    def issue(r, c):
        copy(r, 0, buf0).start()
        copy(r, 1, buf1).start()
        return c

    def drain(r, c):
        copy(r, 0, buf0).wait()
        copy(r, 1, buf1).wait()
        return c

    lax.fori_loop(0, n, issue, 0)
    lax.fori_loop(0, n, drain, 0)
    route = route_ref[...]
    y = route[:, 2:3] * buf0[...] + route[:, 3:4] * buf1[...]
    o_ref[...] = _residual(x_ref[...], y, gp_ref[...], gate_ref[...])


def _combine(ys, pos, route, x, g_post, gate):
    s, d = x.shape
    tm = _tile(s, 256)
    return pl.pallas_call(
        _combine_kernel,
        grid=(s // tm,),
        in_specs=[
            pl.BlockSpec((1, 1, TOP_K * tm), lambda i: (i, 0, 0), memory_space=pltpu.SMEM),
            pl.BlockSpec(memory_space=pl.ANY),
            pl.BlockSpec((tm, LANES), lambda i: (i, 0)),
            pl.BlockSpec((tm, d), lambda i: (i, 0)),
            pl.BlockSpec((1, d), lambda i: (0, 0)),
            pl.BlockSpec((1, d), lambda i: (0, 0)),
        ],
        out_specs=pl.BlockSpec((tm, d), lambda i: (i, 0)),
        out_shape=jax.ShapeDtypeStruct((s, d), F32),
        scratch_shapes=[pltpu.VMEM((tm, d), F32), pltpu.VMEM((tm, d), F32), pltpu.SemaphoreType.DMA(())],
        compiler_params=_cparams(("arbitrary",)),
        name="moe_combine_residual",
    )(pos.reshape(s // tm, 1, TOP_K * tm), ys, route, x, g_post, gate)


MOE_TM = 512


def _moe(h, x, router_w, router_b, wg, wu, wd, g_post, gate):
    s, d = h.shape
    pad = lambda a: jnp.pad(a.astype(F32), ((0, 0), (0, LANES - N_EXPERTS)))
    route, mh = _router(h, pad(router_w), pad(router_b.reshape(1, N_EXPERTS)))
    csum, _ = _cumsum_rows(mh)
    tm = _tile(TOP_K * s, MOE_TM)
    counts = csum[s - 1, :N_EXPERTS].astype(jnp.int32)
    padded = ((counts + tm - 1) // tm) * tm
    ends = jnp.cumsum(padded)
    starts = ends - padded
    ids = route[:, :TOP_K].astype(jnp.int32)
    slot_of = starts[None, :] + (csum[:, :N_EXPERTS] - mh[:, :N_EXPERTS]).astype(jnp.int32)
    pos = jnp.take_along_axis(slot_of, ids, axis=1)
    n_tiles = (TOP_K * s) // tm + N_EXPERTS
    n_used = (ends[N_EXPERTS - 1] // tm).astype(jnp.int32).reshape(1)
    tile_start = jnp.minimum(jnp.arange(n_tiles, dtype=jnp.int32), n_used[0] - 1) * tm
    tile_expert = jnp.sum(tile_start[:, None] >= ends[None, :], axis=1).astype(jnp.int32)
    hs = _dispatch(h, pos, n_tiles * tm)
    act = _ggateup(hs, wg, wu, tile_expert, n_used, tm)
    ys = _gdown(act, wd, tile_expert, n_used, tm)
    return _combine(ys, pos, route, x, g_post, gate)


def kernel(x, c, positions, mod_w, mod_b, norm_g, w_in, b_f, rel_bias, lam, onorm, w_o,
           ffn_wg, ffn_wu, ffn_wd, router_w, router_b, exp_wg, exp_wu, exp_wd):
    b, s, d = x.shape
    assert b == 1 and s % TQB == 0
    depth = mod_w.shape[0]
    xs = x.reshape(s, d)
    pos_col = positions.reshape(s, 1).astype(F32)
    mods = _mods(c, mod_w, mod_b)

    f0 = 3 * WIDTH_A
    scale_ab = HEAD_DIM ** -0.5
    scale_c = DIFF_DIM ** -0.5
    colscale = np.ones((1, w_in.shape[2] - N_HEADS_A), np.float32)
    colscale[0, 0:WIDTH_A] = scale_ab
    colscale[0, f0:f0 + WIDTH_B] = scale_ab
    colscale[0, ROPE_COL_LO:ROPE_COL_LO + WIDTH_C] = scale_c
    colscale = jnp.asarray(colscale)

    for l in range(depth):
        sh_m, sc_m, g_m, sh_f, sc_f, g_f = [mods[l, :, i * d:(i + 1) * d] for i in range(6)]
        ng = lambda i: norm_g[l, i].reshape(1, d).astype(F32)
        w_main = jnp.concatenate([w_in[l, :, :f0], w_in[l, :, f0 + N_HEADS_A:]], axis=1).astype(BF16)
        w_f = jnp.pad(w_in[l, :, f0:f0 + N_HEADS_A], ((0, 0), (0, LANES - N_HEADS_A))).astype(BF16)
        bf = jnp.pad(b_f[l].astype(F32).reshape(1, N_HEADS_A), ((0, 0), (0, LANES - N_HEADS_A)))
        proj, logf = _inproj(xs, ng(0), sc_m, sh_m, w_main, colscale, w_f, bf, pos_col)
        fcol, frow = _cumsum_rows(logf)
        on = lambda i: onorm[l, i].reshape(1, HEAD_DIM).astype(F32)
        lam_init = 0.8 - 0.6 * math.exp(-0.3 * l)
        oa = _attn_a(proj, fcol, frow[:N_HEADS_A].reshape(N_HEADS_A, 1, s), on(0))
        ob = _attn_b(proj, _band_bias_table(rel_bias[l]), on(1))
        oc = _attn_c(proj, lam[l].astype(F32), on(2), lam_init)
        moe = l % 2 == 1
        xs, h2 = _outproj(oa, ob, oc, w_o[l].astype(BF16), xs, ng(1), g_m, ng(2), sc_f, sh_f,
                          F32 if moe else BF16)
        i = l // 2
        if moe:
            xs = _moe(h2, xs, router_w[i], router_b[i], exp_wg[i].astype(BF16), exp_wu[i].astype(BF16),
                      exp_wd[i].astype(BF16), ng(3), g_f)
        else:
            act = _gateup(h2, ffn_wg[i].astype(BF16), ffn_wu[i].astype(BF16))
            xs = _down(act, ffn_wd[i].astype(BF16), xs, ng(3), g_f)
    return xs.reshape(b, s, d)
```

```python
import functools
import math

import numpy as np
import jax
import jax.numpy as jnp
from jax import lax
from jax.experimental import pallas as pl
from jax.experimental.pallas import tpu as pltpu

F32 = jnp.float32
BF16 = jnp.bfloat16

CHUNK = 64
HEAD_DIM = 128
N_HEADS_A = 6
N_HEADS_B = 6
N_HEADS_C = 4
WIDTH_A = N_HEADS_A * HEAD_DIM
WIDTH_B = N_HEADS_B * HEAD_DIM
WIDTH_C = N_HEADS_C * HEAD_DIM
DIFF_DIM = HEAD_DIM // 2
LEFT_CHUNKS = 8
REL_CLIP = 256
ROPE_THETA = 500000.0
ROPE_DIM = DIFF_DIM // 4
N_EXPERTS = 8
TOP_K = 2
EPS = 1e-6
NEG_INF = -1e30
LOG2E = math.log2(math.e)

LANES = 128
VMEM_LIMIT = 56 * 1024 * 1024


def _cparams(sem):
    return pltpu.CompilerParams(dimension_semantics=sem, vmem_limit_bytes=VMEM_LIMIT)


def _tile(n, pref):
    t = min(n, pref)
    while n % t:
        t //= 2
    return t


def _rms(x, g):
    return (x * lax.rsqrt(jnp.mean(x * x, axis=-1, keepdims=True) + EPS)) * g


def _mods_kernel(c_ref, w_ref, b_ref, o_ref):
    c = c_ref[...]
    cond = c * jax.nn.sigmoid(c)
    o_ref[...] = jnp.sum(w_ref[...] * cond, axis=0, keepdims=True) + b_ref[...]


def _mods(c, mod_w, mod_b):
    depth, d, n = mod_w.shape
    tn = _tile(n, 1024)
    return pl.pallas_call(
        _mods_kernel,
        grid=(depth, n // tn),
        in_specs=[
            pl.BlockSpec((d, 1), lambda l, j: (0, 0)),
            pl.BlockSpec((None, d, tn), lambda l, j: (l, 0, j)),
            pl.BlockSpec((None, 1, tn), lambda l, j: (l, 0, j)),
        ],
        out_specs=pl.BlockSpec((None, 1, tn), lambda l, j: (l, 0, j)),
        out_shape=jax.ShapeDtypeStruct((depth, 1, n), F32),
        compiler_params=_cparams(("arbitrary", "arbitrary")),
        name="adaln_mods",
    )(c.reshape(d, 1), mod_w, mod_b.reshape(depth, 1, n))


ROPE_COL_LO = 3 * WIDTH_A + 3 * WIDTH_B
ROPE_COL_HI = ROPE_COL_LO + 2 * WIDTH_C
INPROJ_TN = 512


def _inproj_kernel(x_ref, g_ref, sc_ref, sh_ref, w_ref, cs_ref, wf_ref, bf_ref, pos_ref,
                   inv_ref, m1_ref, m2_ref, o_ref, lf_ref, h_scr, cos_scr, sin_scr):
    j = pl.program_id(1)

    @pl.when(j == 0)
    def _():
        h = _rms(x_ref[...], g_ref[...]) * (1.0 + sc_ref[...]) + sh_ref[...]
        hb = h.astype(BF16)
        h_scr[...] = hb
        fa = jnp.dot(hb, wf_ref[...], preferred_element_type=F32) + bf_ref[...]
        lf_ref[...] = jax.nn.log_sigmoid(fa)
        ang = pos_ref[...] * inv_ref[...]
        cos_scr[...] = jnp.cos(ang)
        sin_scr[...] = jnp.sin(ang)

    y = jnp.dot(h_scr[...], w_ref[...], preferred_element_type=F32) * cs_ref[...]
    is_rope = jnp.logical_and(j >= ROPE_COL_LO // INPROJ_TN, j < ROPE_COL_HI // INPROJ_TN)

    @pl.when(is_rope)
    def _():
        c = cos_scr[...]
        s = sin_scr[...]
        m1 = m1_ref[...]
        m2 = m2_ref[...]
        for hh in range(INPROJ_TN // LANES):
            yh = y[:, hh * LANES:(hh + 1) * LANES]
            up = pltpu.roll(yh, LANES - ROPE_DIM // 2, 1)
            dn = pltpu.roll(yh, ROPE_DIM // 2, 1)
            o_ref[:, hh * LANES:(hh + 1) * LANES] = (yh * c + s * (m1 * up + m2 * dn)).astype(BF16)

    @pl.when(jnp.logical_not(is_rope))
    def _():
        o_ref[...] = y.astype(BF16)


def _rope_lane_tables():
    lane = np.arange(LANES)
    r = lane % DIFF_DIM
    half = ROPE_DIM // 2
    inv = np.where(r < ROPE_DIM, ROPE_THETA ** (-((r % half) * 2.0 / ROPE_DIM)), 0.0)
    m1 = np.where(r < half, -1.0, 0.0)
    m2 = np.where((r >= half) & (r < ROPE_DIM), 1.0, 0.0)
    f = lambda a: jnp.asarray(a.reshape(1, LANES), F32)
    return f(inv), f(m1), f(m2)


def _inproj(x, g, sc, sh, w_main, colscale, w_f, b_f, pos_col):
    s, d = x.shape
    n = w_main.shape[1]
    tm = _tile(s, 1024)
    tn = INPROJ_TN
    inv, m1, m2 = _rope_lane_tables()
    row = lambda i, j: (i, 0)
    const = lambda i, j: (0, 0)
    return pl.pallas_call(
        _inproj_kernel,
        grid=(s // tm, n // tn),
        in_specs=[
            pl.BlockSpec((tm, d), row),
            pl.BlockSpec((1, d), const),
            pl.BlockSpec((1, d), const),
            pl.BlockSpec((1, d), const),
            pl.BlockSpec((d, tn), lambda i, j: (0, j)),
            pl.BlockSpec((1, tn), lambda i, j: (0, j)),
            pl.BlockSpec((d, LANES), const),
            pl.BlockSpec((1, LANES), const),
            pl.BlockSpec((tm, 1), row),
            pl.BlockSpec((1, LANES), const),
            pl.BlockSpec((1, LANES), const),
            pl.BlockSpec((1, LANES), const),
        ],
        out_specs=[
            pl.BlockSpec((tm, tn), lambda i, j: (i, j)),
            pl.BlockSpec((tm, LANES), row),
        ],
        out_shape=[
            jax.ShapeDtypeStruct((s, n), BF16),
            jax.ShapeDtypeStruct((s, LANES), F32),
        ],
        scratch_shapes=[
            pltpu.VMEM((tm, d), BF16),
            pltpu.VMEM((tm, LANES), F32),
            pltpu.VMEM((tm, LANES), F32),
        ],
        compiler_params=_cparams(("arbitrary", "arbitrary")),
        name="inproj",
    )(x, g, sc, sh, w_main, colscale, w_f, b_f, pos_col, inv, m1, m2)


DECAY_PIECES = 3


def _split3(x):
    x1 = x.astype(BF16)
    r1 = x - x1.astype(F32)
    x2 = r1.astype(BF16)
    return x1, x2, (r1 - x2.astype(F32)).astype(BF16)


def _cumsum_kernel(x_ref, o_ref, *rest, decay_heads):
    carry = rest[-1]

    @pl.when(pl.program_id(0) == 0)
    def _():
        carry[...] = jnp.zeros_like(carry)

    x = x_ref[...]
    tb = x.shape[0]
    r = lax.broadcasted_iota(jnp.int32, (tb, tb), 0)
    c = lax.broadcasted_iota(jnp.int32, (tb, tb), 1)
    tri = jnp.where(c <= r, 1.0, 0.0).astype(BF16)
    y = sum(jnp.dot(tri, piece, preferred_element_type=F32) for piece in _split3(x)) + carry[...]
    o_ref[...] = y
    carry[...] = y[tb - 1:tb, :]
    if decay_heads:
        dec_ref = rest[0]
        lane = lax.broadcasted_iota(jnp.int32, (tb, LANES), 1)
        for h in range(decay_heads):
            hi, mid, lo = [p.astype(F32) for p in _split3(-LOG2E * y[:, h:h + 1])]
            cols = jnp.where(lane == 0, hi, jnp.where(lane == 1, mid, jnp.where(lane == 2, lo, 0.0)))
            dec_ref[h] = cols.astype(BF16)


def _cumsum_rows(x, decay_heads=0):
    s = x.shape[0]
    tb = _tile(s, 256)
    out_specs = [pl.BlockSpec((tb, LANES), lambda i: (i, 0))]
    out_shape = [jax.ShapeDtypeStruct((s, LANES), F32)]
    if decay_heads:
        out_specs.append(pl.BlockSpec((decay_heads, tb, LANES), lambda i: (0, i, 0)))
        out_shape.append(jax.ShapeDtypeStruct((decay_heads, s, LANES), BF16))
    return pl.pallas_call(
        functools.partial(_cumsum_kernel, decay_heads=decay_heads),
        grid=(s // tb,),
        in_specs=[pl.BlockSpec((tb, LANES), lambda i: (i, 0))],
        out_specs=out_specs,
        out_shape=out_shape,
        scratch_shapes=[pltpu.VMEM((1, LANES), F32)],
        compiler_params=_cparams(("arbitrary",)),
        name="cumsum_rows",
    )(x)


def _qk(q, k):
    return lax.dot_general(q, k, (((1,), (1,)), ((), ())), preferred_element_type=F32)


def _lane_fold(p):
    return functools.reduce(jnp.add, [p[:, i * LANES:(i + 1) * LANES] for i in range(p.shape[1] // LANES)])


def _online_update(s, v, m, l, acc):
    m_new = jnp.maximum(m, jnp.max(s, axis=-1, keepdims=True))
    p = jnp.exp2(s - m_new)
    alpha = jnp.exp2(m - m_new)
    l = alpha * l + _lane_fold(p)
    acc = alpha * acc + jnp.dot(p.astype(BF16), v, preferred_element_type=F32)
    return m_new, l, acc


def _softmax_state(tq):
    return (jnp.full((tq, 1), NEG_INF, F32), jnp.zeros((tq, LANES), F32), jnp.zeros((tq, HEAD_DIM), F32))


def _normalize(l, acc):
    return acc / jnp.sum(l, axis=-1, keepdims=True)


ATTN_TQ = 1024
ATTN_TK = 1024


def _attn_a_kernel(q_ref, k_ref, v_ref, fa_ref, g_ref, o_ref, *, tq, tk):
    qi = pl.program_id(1)
    lane = lax.broadcasted_iota(jnp.int32, (tq, LANES), 1)
    ones = jnp.where(lane < DECAY_PIECES, 1.0, 0.0).astype(BF16)
    q = jnp.concatenate([q_ref[...], ones], axis=1)

    def block(kb, carry, masked):
        for j in range(tq // tk):
            start = pl.multiple_of(kb * tq + j * tk, tk)
            k = jnp.concatenate([k_ref[pl.ds(start, tk), :], fa_ref[pl.ds(start, tk), :]], axis=1)
            s = _qk(q, k)
            if masked:
                r = lax.broadcasted_iota(jnp.int32, (tq, tk), 0)
                c = lax.broadcasted_iota(jnp.int32, (tq, tk), 1) + j * tk
                s = jnp.where(c <= r, s, NEG_INF)
            carry = _online_update(s, v_ref[pl.ds(start, tk), :], *carry)
        return carry

    carry = lax.fori_loop(0, qi, lambda kb, cr: block(kb, cr, False), _softmax_state(tq))
    _, l, acc = block(qi, carry, True)
    o_ref[...] = _rms(_normalize(l, acc), g_ref[...]).astype(BF16)


def _attn_a(proj, decay, g):
    s = proj.shape[0]
    tq = _tile(s, ATTN_TQ)
    nh = N_HEADS_A
    kern = functools.partial(_attn_a_kernel, tq=tq, tk=_tile(tq, ATTN_TK))
    return pl.pallas_call(
        kern,
        grid=(nh, s // tq),
        in_specs=[
            pl.BlockSpec((tq, HEAD_DIM), lambda h, i: (i, h)),
            pl.BlockSpec((s, HEAD_DIM), lambda h, i: (0, nh + h)),
            pl.BlockSpec((s, HEAD_DIM), lambda h, i: (0, 2 * nh + h)),
            pl.BlockSpec((None, s, LANES), lambda h, i: (h, 0, 0)),
            pl.BlockSpec((1, HEAD_DIM), lambda h, i: (0, 0)),
        ],
        out_specs=pl.BlockSpec((tq, HEAD_DIM), lambda h, i: (i, h)),
        out_shape=jax.ShapeDtypeStruct((s, WIDTH_A), BF16),
        compiler_params=_cparams(("arbitrary", "arbitrary")),
        name="attn_forget",
    )(proj, proj, proj, decay, g)


TQB = 256
BAND = LEFT_CHUNKS * CHUNK
NWIN = (BAND + TQB) // TQB


def _attn_b_kernel(q_ref, k_ref, v_ref, bias_ref, g_ref, o_ref):
    qi = pl.program_id(1)
    q = q_ref[...]
    ss = []
    for w in range(NWIN):
        kb = qi - (NWIN - 1) + w
        start = pl.multiple_of(jnp.maximum(kb, 0) * TQB, TQB)
        sw = _qk(q, k_ref[pl.ds(start, TQB), :]) + bias_ref[:, w * TQB:(w + 1) * TQB]
        ss.append(jnp.where(kb >= 0, sw, NEG_INF))
    m = functools.reduce(jnp.maximum, [jnp.max(sw, axis=-1, keepdims=True) for sw in ss])
    l = jnp.zeros((TQB, LANES), F32)
    acc = jnp.zeros((TQB, HEAD_DIM), F32)
    for w in range(NWIN):
        kb = qi - (NWIN - 1) + w
        start = pl.multiple_of(jnp.maximum(kb, 0) * TQB, TQB)
        p = jnp.exp2(ss[w] - m)
        l = l + _lane_fold(p)
        acc = acc + jnp.dot(p.astype(BF16), v_ref[pl.ds(start, TQB), :], preferred_element_type=F32)
    o_ref[...] = _rms(_normalize(l, acc), g_ref[...]).astype(BF16)


def _band_bias_table(rel_bias):
    w = NWIN * TQB
    n = TQB + w
    u = np.arange(n) - (TQB - 1)
    dist = (NWIN - 1) * TQB - u
    idx = np.clip(dist, -REL_CLIP, REL_CLIP) + REL_CLIP
    diag = rel_bias.astype(F32)[:, idx] * LOG2E
    diag = jnp.roll(diag, -(TQB - 1), axis=1)
    nh = rel_bias.shape[0]
    skew = jnp.tile(diag, (1, TQB))[:, :TQB * (n - 1)].reshape(nh, TQB, n - 1)[:, :, :w]
    qpos = np.arange(TQB)[:, None]
    kpos = np.arange(w)[None, :] - (NWIN - 1) * TQB
    qc = qpos // CHUNK
    kc = np.floor_divide(kpos, CHUNK)
    valid = (kc <= qc) & (kc >= qc - LEFT_CHUNKS)
    return jnp.where(jnp.asarray(valid)[None], skew, NEG_INF)


def _attn_b(proj, bias_tbl, g):
    s = proj.shape[0]
    nh = N_HEADS_B
    c0 = 3 * N_HEADS_A
    return pl.pallas_call(
        _attn_b_kernel,
        grid=(nh, s // TQB),
        in_specs=[
            pl.BlockSpec((TQB, HEAD_DIM), lambda h, i: (i, c0 + h)),
            pl.BlockSpec((s, HEAD_DIM), lambda h, i: (0, c0 + nh + h)),
            pl.BlockSpec((s, HEAD_DIM), lambda h, i: (0, c0 + 2 * nh + h)),
            pl.BlockSpec((None, TQB, NWIN * TQB), lambda h, i: (h, 0, 0)),
            pl.BlockSpec((1, HEAD_DIM), lambda h, i: (0, 0)),
        ],
        out_specs=pl.BlockSpec((TQB, HEAD_DIM), lambda h, i: (i, h)),
        out_shape=jax.ShapeDtypeStruct((s, WIDTH_B), BF16),
        compiler_params=_cparams(("arbitrary", "arbitrary")),
        name="attn_band",
    )(proj, proj, proj, bias_tbl, g)


def _attn_c_kernel(q_ref, k_ref, v_ref, lam_ref, g_ref, o_ref, *, tq, tk, lam_init):
    qi = pl.program_id(1)
    q = q_ref[...]
    lane = lax.broadcasted_iota(jnp.int32, (1, HEAD_DIM), 1)
    zero = jnp.zeros_like(q)
    q0 = jnp.where(lane < DIFF_DIM, q, zero)
    q1 = jnp.where(lane >= DIFF_DIM, q, zero)

    def block(kb, carry, masked):
        st0, st1 = carry[:3], carry[3:]
        for j in range(tq // tk):
            start = pl.multiple_of(kb * tq + j * tk, tk)
            k = k_ref[pl.ds(start, tk), :]
            v = v_ref[pl.ds(start, tk), :]
            s0 = _qk(q0, k)
            s1 = _qk(q1, k)
            if masked:
                r = lax.broadcasted_iota(jnp.int32, (tq, tk), 0) // CHUNK
                c = (lax.broadcasted_iota(jnp.int32, (tq, tk), 1) + j * tk) // CHUNK
                ok = c <= r
                s0 = jnp.where(ok, s0, NEG_INF)
                s1 = jnp.where(ok, s1, NEG_INF)
            st0 = _online_update(s0, v, *st0)
            st1 = _online_update(s1, v, *st1)
        return st0 + st1

    st = _softmax_state(tq)
    carry = lax.fori_loop(0, qi, lambda kb, cr: block(kb, cr, False), st + st)
    _, l0, a0, _, l1, a1 = block(qi, carry, True)
    lam = lam_ref[...]
    lam_full = (jnp.exp(jnp.sum(lam[0:1] * lam[1:2], axis=-1, keepdims=True))
                - jnp.exp(jnp.sum(lam[2:3] * lam[3:4], axis=-1, keepdims=True)) + lam_init)
    o = _normalize(l0, a0) - lam_full * _normalize(l1, a1)
    o_ref[...] = (_rms(o, g_ref[...]) * (1.0 - lam_init)).astype(BF16)


def _attn_c(proj, lam, g, lam_init):
    s = proj.shape[0]
    tq = _tile(s, ATTN_TQ)
    nh = N_HEADS_C
    c0 = 3 * N_HEADS_A + 3 * N_HEADS_B
    kern = functools.partial(_attn_c_kernel, tq=tq, tk=_tile(tq, ATTN_TK), lam_init=lam_init)
    return pl.pallas_call(
        kern,
        grid=(nh, s // tq),
        in_specs=[
            pl.BlockSpec((tq, HEAD_DIM), lambda h, i: (i, c0 + h)),
            pl.BlockSpec((s, HEAD_DIM), lambda h, i: (0, c0 + nh + h)),
            pl.BlockSpec((s, HEAD_DIM), lambda h, i: (0, c0 + 2 * nh + h)),
            pl.BlockSpec((4, DIFF_DIM), lambda h, i: (0, 0)),
            pl.BlockSpec((1, HEAD_DIM), lambda h, i: (0, 0)),
        ],
        out_specs=pl.BlockSpec((tq, HEAD_DIM), lambda h, i: (i, h)),
        out_shape=jax.ShapeDtypeStruct((s, WIDTH_C), BF16),
        compiler_params=_cparams(("arbitrary", "arbitrary")),
        name="attn_diff",
    )(proj, proj, proj, lam, g)


def _residual(x, y, g_post, gate):
    return x + gate * _rms(y, g_post)


def _outproj_kernel(oa_ref, ob_ref, oc_ref, wa_ref, wb_ref, wc_ref, x_ref, gp_ref, gate_ref,
                    gn_ref, sc_ref, sh_ref, xo_ref, ho_ref):
    y = (jnp.dot(oa_ref[...], wa_ref[...], preferred_element_type=F32)
         + jnp.dot(ob_ref[...], wb_ref[...], preferred_element_type=F32)
         + jnp.dot(oc_ref[...], wc_ref[...], preferred_element_type=F32))
    xn = _residual(x_ref[...], y, gp_ref[...], gate_ref[...])
    xo_ref[...] = xn
    ho_ref[...] = (_rms(xn, gn_ref[...]) * (1.0 + sc_ref[...]) + sh_ref[...]).astype(ho_ref.dtype)


def _outproj(oa, ob, oc, w_o, x, g_post, gate, g_next, sc, sh, h_dtype):
    s, d = x.shape
    tm = _tile(s, 256)
    row = lambda i: (i, 0)
    const = lambda i: (0, 0)
    assert WIDTH_A == WIDTH_B and (WIDTH_A + WIDTH_B) % WIDTH_C == 0
    return pl.pallas_call(
        _outproj_kernel,
        grid=(s // tm,),
        in_specs=[
            pl.BlockSpec((tm, WIDTH_A), row),
            pl.BlockSpec((tm, WIDTH_B), row),
            pl.BlockSpec((tm, WIDTH_C), row),
            pl.BlockSpec((WIDTH_A, d), lambda i: (0, 0)),
            pl.BlockSpec((WIDTH_B, d), lambda i: (1, 0)),
            pl.BlockSpec((WIDTH_C, d), lambda i: ((WIDTH_A + WIDTH_B) // WIDTH_C, 0)),
            pl.BlockSpec((tm, d), row),
            pl.BlockSpec((1, d), const),
            pl.BlockSpec((1, d), const),
            pl.BlockSpec((1, d), const),
            pl.BlockSpec((1, d), const),
            pl.BlockSpec((1, d), const),
        ],
        out_specs=[pl.BlockSpec((tm, d), row), pl.BlockSpec((tm, d), row)],
        out_shape=[jax.ShapeDtypeStruct((s, d), F32), jax.ShapeDtypeStruct((s, d), h_dtype)],
        compiler_params=_cparams(("arbitrary",)),
        name="outproj_residual",
    )(oa, ob, oc, w_o, w_o, w_o, x, g_post, gate, g_next, sc, sh)


def _gateup_kernel(h_ref, wg_ref, wu_ref, o_ref):
    h = h_ref[...]
    a = jnp.dot(h, wg_ref[...], preferred_element_type=F32)
    u = jnp.dot(h, wu_ref[...], preferred_element_type=F32)
    o_ref[...] = ((a * jax.nn.sigmoid(a)) * u).astype(BF16)


def _gateup(h, wg, wu):
    s, d = h.shape
    f = wg.shape[1]
    tm = _tile(s, 1024)
    tf = _tile(f, 512)
    return pl.pallas_call(
        _gateup_kernel,
        grid=(s // tm, f // tf),
        in_specs=[
            pl.BlockSpec((tm, d), lambda i, j: (i, 0)),
            pl.BlockSpec((d, tf), lambda i, j: (0, j)),
            pl.BlockSpec((d, tf), lambda i, j: (0, j)),
        ],
        out_specs=pl.BlockSpec((tm, tf), lambda i, j: (i, j)),
        out_shape=jax.ShapeDtypeStruct((s, f), BF16),
        compiler_params=_cparams(("arbitrary", "arbitrary")),
        name="ffn_gateup",
    )(h, wg, wu)


def _down_kernel(a_ref, w_ref, x_ref, gp_ref, gate_ref, o_ref, acc):
    k = pl.program_id(1)

    @pl.when(k == 0)
    def _():
        acc[...] = jnp.zeros_like(acc)

    acc[...] += jnp.dot(a_ref[...], w_ref[...], preferred_element_type=F32)

    @pl.when(k == pl.num_programs(1) - 1)
    def _():
        o_ref[...] = _residual(x_ref[...], acc[...], gp_ref[...], gate_ref[...])


def _down(act, wd, x, g_post, gate):
    s, f = act.shape
    d = wd.shape[1]
    tm = _tile(s, 512)
    tk = _tile(f, 1024)
    return pl.pallas_call(
        _down_kernel,
        grid=(s // tm, f // tk),
        in_specs=[
            pl.BlockSpec((tm, tk), lambda i, k: (i, k)),
            pl.BlockSpec((tk, d), lambda i, k: (k, 0)),
            pl.BlockSpec((tm, d), lambda i, k: (i, 0)),
            pl.BlockSpec((1, d), lambda i, k: (0, 0)),
            pl.BlockSpec((1, d), lambda i, k: (0, 0)),
        ],
        out_specs=pl.BlockSpec((tm, d), lambda i, k: (i, 0)),
        out_shape=jax.ShapeDtypeStruct((s, d), F32),
        scratch_shapes=[pltpu.VMEM((tm, d), F32)],
        compiler_params=_cparams(("arbitrary", "arbitrary")),
        name="ffn_down_residual",
    )(act, wd, x, g_post, gate)


def _split2(a):
    hi = a.astype(BF16)
    return hi, (a - hi.astype(F32)).astype(BF16)


def _router_kernel(h_ref, w_ref, b_ref, route_ref, mh_ref):
    hh, hl = _split2(h_ref[...])
    wh, wl = _split2(w_ref[...])
    logits = (jnp.dot(hh, wh, preferred_element_type=F32)
              + jnp.dot(hh, wl, preferred_element_type=F32)
              + jnp.dot(hl, wh, preferred_element_type=F32)) + b_ref[...]
    lane = lax.broadcasted_iota(jnp.int32, logits.shape, 1)
    logits = jnp.where(lane < N_EXPERTS, logits, NEG_INF)
    v1 = jnp.max(logits, axis=-1, keepdims=True)
    i1 = jnp.min(jnp.where(logits == v1, lane, LANES), axis=-1, keepdims=True)
    rest = jnp.where(lane == i1, NEG_INF, logits)
    v2 = jnp.max(rest, axis=-1, keepdims=True)
    i2 = jnp.min(jnp.where(rest == v2, lane, LANES), axis=-1, keepdims=True)
    e = jnp.exp(v2 - v1)
    g1 = 1.0 / (1.0 + e)
    g2 = e / (1.0 + e)
    route_ref[...] = jnp.where(lane == 0, i1.astype(F32),
                               jnp.where(lane == 1, i2.astype(F32),
                                         jnp.where(lane == 2, g1, jnp.where(lane == 3, g2, 0.0))))
    mh_ref[...] = jnp.where(jnp.logical_or(lane == i1, lane == i2), 1.0, 0.0)


def _router(h, rw_pad, rb_pad):
    s, d = h.shape
    tm = _tile(s, 512)
    return pl.pallas_call(
        _router_kernel,
        grid=(s // tm,),
        in_specs=[
            pl.BlockSpec((tm, d), lambda i: (i, 0)),
            pl.BlockSpec((d, LANES), lambda i: (0, 0)),
            pl.BlockSpec((1, LANES), lambda i: (0, 0)),
        ],
        out_specs=[pl.BlockSpec((tm, LANES), lambda i: (i, 0)),
                   pl.BlockSpec((tm, LANES), lambda i: (i, 0))],
        out_shape=[jax.ShapeDtypeStruct((s, LANES), F32), jax.ShapeDtypeStruct((s, LANES), F32)],
        compiler_params=_cparams(("arbitrary",)),
        name="moe_router",
    )(h, rw_pad, rb_pad)


def _dispatch_kernel(pos_ref, h_ref, init_ref, o_ref, sem):
    del init_ref
    n = h_ref.shape[0]

    def copy(r, k):
        return pltpu.make_async_copy(h_ref.at[pl.ds(r, 1)], o_ref.at[pl.ds(pos_ref[0, 0, 2 * r + k], 1)], sem)

    def issue(r, c):
        copy(r, 0).start()
        copy(r, 1).start()
        return c

    def drain(r, c):
        copy(r, 0).wait()
        copy(r, 1).wait()
        return c

    lax.fori_loop(0, n, issue, 0)
    lax.fori_loop(0, n, drain, 0)


def _dispatch(h, pos, n_slots):
    s, d = h.shape
    tm = _tile(s, 256)
    init = jnp.zeros((n_slots, d), F32)
    return pl.pallas_call(
        _dispatch_kernel,
        grid=(s // tm,),
        in_specs=[
            pl.BlockSpec((1, 1, TOP_K * tm), lambda i: (i, 0, 0), memory_space=pltpu.SMEM),
            pl.BlockSpec((tm, d), lambda i: (i, 0)),
            pl.BlockSpec(memory_space=pl.ANY),
        ],
        out_specs=pl.BlockSpec(memory_space=pl.ANY),
        out_shape=jax.ShapeDtypeStruct((n_slots, d), F32),
        scratch_shapes=[pltpu.SemaphoreType.DMA(())],
        input_output_aliases={2: 0},
        compiler_params=_cparams(("arbitrary",)),
        name="moe_dispatch",
    )(pos.reshape(s // tm, 1, TOP_K * tm), h, init)


def _ggateup_kernel(te_ref, nu_ref, h_ref, wg_ref, wu_ref, o_ref, hb_scr):
    i = pl.program_id(0)
    j = pl.program_id(1)

    @pl.when(i < nu_ref[0])
    def _():
        @pl.when(j == 0)
        def _():
            hb_scr[...] = h_ref[...].astype(BF16)

        h = hb_scr[...]
        a = jnp.dot(h, wg_ref[...], preferred_element_type=F32)
        u = jnp.dot(h, wu_ref[...], preferred_element_type=F32)
        o_ref[...] = ((a * jax.nn.sigmoid(a)) * u).astype(BF16)

    @pl.when(i >= nu_ref[0])
    def _():
        o_ref[...] = jnp.zeros_like(o_ref)


def _ggateup(hs, wg, wu, tile_expert, n_used, tm):
    p, d = hs.shape
    f = wg.shape[2]
    tf = _tile(f, 512)
    return pl.pallas_call(
        _ggateup_kernel,
        grid_spec=pltpu.PrefetchScalarGridSpec(
            num_scalar_prefetch=2,
            grid=(p // tm, f // tf),
            in_specs=[
                pl.BlockSpec((tm, d), lambda i, j, te, nu: (i, 0)),
                pl.BlockSpec((None, d, tf), lambda i, j, te, nu: (te[i], 0, j)),
                pl.BlockSpec((None, d, tf), lambda i, j, te, nu: (te[i], 0, j)),
            ],
            out_specs=pl.BlockSpec((tm, tf), lambda i, j, te, nu: (i, j)),
            scratch_shapes=[pltpu.VMEM((tm, d), BF16)],
        ),
        out_shape=jax.ShapeDtypeStruct((p, f), BF16),
        compiler_params=_cparams(("arbitrary", "arbitrary")),
        name="moe_gateup",
    )(tile_expert, n_used, hs, wg, wu)


def _gdown_kernel(te_ref, nu_ref, a_ref, w_ref, o_ref, acc):
    i = pl.program_id(0)
    k = pl.program_id(1)
    last = k == pl.num_programs(1) - 1

    @pl.when(i < nu_ref[0])
    def _():
        @pl.when(k == 0)
        def _():
            acc[...] = jnp.zeros_like(acc)

        acc[...] += jnp.dot(a_ref[...], w_ref[...], preferred_element_type=F32)

        @pl.when(last)
        def _():
            o_ref[...] = acc[...]

    @pl.when(jnp.logical_and(i >= nu_ref[0], last))
    def _():
        o_ref[...] = jnp.zeros_like(o_ref)


def _gdown(act, wd, tile_expert, n_used, tm):
    p, f = act.shape
    d = wd.shape[2]
    tk = _tile(f, 1024)
    return pl.pallas_call(
        _gdown_kernel,
        grid_spec=pltpu.PrefetchScalarGridSpec(
            num_scalar_prefetch=2,
            grid=(p // tm, f // tk),
            in_specs=[
                pl.BlockSpec((tm, tk), lambda i, k, te, nu: (i, k)),
                pl.BlockSpec((None, tk, d), lambda i, k, te, nu: (te[i], k, 0)),
            ],
            out_specs=pl.BlockSpec((tm, d), lambda i, k, te, nu: (i, 0)),
            scratch_shapes=[pltpu.VMEM((tm, d), F32)],
        ),
        out_shape=jax.ShapeDtypeStruct((p, d), F32),
        compiler_params=_cparams(("arbitrary", "arbitrary")),
        name="moe_down",
    )(tile_expert, n_used, act, wd)


def _combine_kernel(pos_ref, y_ref, route_ref, x_ref, gp_ref, gate_ref, o_ref, buf0, buf1, sem):
    n = x_ref.shape[0]

    def copy(r, k, buf):
        return pltpu.make_async_copy(y_ref.at[pl.ds(pos_ref[0, 0, 2 * r + k], 1)], buf.at[pl.ds(r, 1)], sem)

    def issue(r, c):
        copy(r, 0, buf0).start()
        copy(r, 1, buf1).start()
        return c

    def drain(r, c):
        copy(r, 0, buf0).wait()
        copy(r, 1, buf1).wait()
        return c

    lax.fori_loop(0, n, issue, 0)
    lax.fori_loop(0, n, drain, 0)
    route = route_ref[...]
    y = route[:, 2:3] * buf0[...] + route[:, 3:4] * buf1[...]
    o_ref[...] = _residual(x_ref[...], y, gp_ref[...], gate_ref[...])


def _combine(ys, pos, route, x, g_post, gate):
    s, d = x.shape
    tm = _tile(s, 256)
    return pl.pallas_call(
        _combine_kernel,
        grid=(s // tm,),
        in_specs=[
            pl.BlockSpec((1, 1, TOP_K * tm), lambda i: (i, 0, 0), memory_space=pltpu.SMEM),
            pl.BlockSpec(memory_space=pl.ANY),
            pl.BlockSpec((tm, LANES), lambda i: (i, 0)),
            pl.BlockSpec((tm, d), lambda i: (i, 0)),
            pl.BlockSpec((1, d), lambda i: (0, 0)),
            pl.BlockSpec((1, d), lambda i: (0, 0)),
        ],
        out_specs=pl.BlockSpec((tm, d), lambda i: (i, 0)),
        out_shape=jax.ShapeDtypeStruct((s, d), F32),
        scratch_shapes=[pltpu.VMEM((tm, d), F32), pltpu.VMEM((tm, d), F32), pltpu.SemaphoreType.DMA(())],
        compiler_params=_cparams(("arbitrary",)),
        name="moe_combine_residual",
    )(pos.reshape(s // tm, 1, TOP_K * tm), ys, route, x, g_post, gate)


MOE_TM = 512


def _moe(h, x, router_w, router_b, wg, wu, wd, g_post, gate):
    s, d = h.shape
    pad = lambda a: jnp.pad(a.astype(F32), ((0, 0), (0, LANES - N_EXPERTS)))
    route, mh = _router(h, pad(router_w), pad(router_b.reshape(1, N_EXPERTS)))
    csum, = _cumsum_rows(mh)
    tm = _tile(TOP_K * s, MOE_TM)
    counts = csum[s - 1, :N_EXPERTS].astype(jnp.int32)
    padded = ((counts + tm - 1) // tm) * tm
    ends = jnp.cumsum(padded)
    starts = ends - padded
    ids = route[:, :TOP_K].astype(jnp.int32)
    slot_of = starts[None, :] + (csum[:, :N_EXPERTS] - mh[:, :N_EXPERTS]).astype(jnp.int32)
    pos = jnp.take_along_axis(slot_of, ids, axis=1)
    n_tiles = (TOP_K * s) // tm + N_EXPERTS
    n_used = (ends[N_EXPERTS - 1] // tm).astype(jnp.int32).reshape(1)
    tile_start = jnp.minimum(jnp.arange(n_tiles, dtype=jnp.int32), n_used[0] - 1) * tm
    tile_expert = jnp.sum(tile_start[:, None] >= ends[None, :], axis=1).astype(jnp.int32)
    hs = _dispatch(h, pos, n_tiles * tm)
    act = _ggateup(hs, wg, wu, tile_expert, n_used, tm)
    ys = _gdown(act, wd, tile_expert, n_used, tm)
    return _combine(ys, pos, route, x, g_post, gate)


def kernel(x, c, positions, mod_w, mod_b, norm_g, w_in, b_f, rel_bias, lam, onorm, w_o,
           ffn_wg, ffn_wu, ffn_wd, router_w, router_b, exp_wg, exp_wu, exp_wd):
    b, s, d = x.shape
    assert b == 1 and s % TQB == 0
    depth = mod_w.shape[0]
    xs = x.reshape(s, d)
    pos_col = positions.reshape(s, 1).astype(F32)
    mods = _mods(c, mod_w, mod_b)

    f0 = 3 * WIDTH_A
    scale_ab = HEAD_DIM ** -0.5 * LOG2E
    scale_c = DIFF_DIM ** -0.5 * LOG2E
    colscale = np.ones((1, w_in.shape[2] - N_HEADS_A), np.float32)
    colscale[0, 0:WIDTH_A] = scale_ab
    colscale[0, f0:f0 + WIDTH_B] = scale_ab
    colscale[0, ROPE_COL_LO:ROPE_COL_LO + WIDTH_C] = scale_c
    colscale = jnp.asarray(colscale)

    for l in range(depth):
        sh_m, sc_m, g_m, sh_f, sc_f, g_f = [mods[l, :, i * d:(i + 1) * d] for i in range(6)]
        ng = lambda i: norm_g[l, i].reshape(1, d).astype(F32)
        w_main = jnp.concatenate([w_in[l, :, :f0], w_in[l, :, f0 + N_HEADS_A:]], axis=1).astype(BF16)
        w_f = jnp.pad(w_in[l, :, f0:f0 + N_HEADS_A], ((0, 0), (0, LANES - N_HEADS_A))).astype(BF16)
        bf = jnp.pad(b_f[l].astype(F32).reshape(1, N_HEADS_A), ((0, 0), (0, LANES - N_HEADS_A)))
        proj, logf = _inproj(xs, ng(0), sc_m, sh_m, w_main, colscale, w_f, bf, pos_col)
        _, decay = _cumsum_rows(logf, decay_heads=N_HEADS_A)
        on = lambda i: onorm[l, i].reshape(1, HEAD_DIM).astype(F32)
        lam_init = 0.8 - 0.6 * math.exp(-0.3 * l)
        oa = _attn_a(proj, decay, on(0))
        ob = _attn_b(proj, _band_bias_table(rel_bias[l]), on(1))
        oc = _attn_c(proj, lam[l].astype(F32), on(2), lam_init)
        moe = l % 2 == 1
        xs, h2 = _outproj(oa, ob, oc, w_o[l].astype(BF16), xs, ng(1), g_m, ng(2), sc_f, sh_f,
                          F32 if moe else BF16)
        i = l // 2
        if moe:
            xs = _moe(h2, xs, router_w[i], router_b[i], exp_wg[i].astype(BF16), exp_wu[i].astype(BF16),
                      exp_wd[i].astype(BF16), ng(3), g_f)
        else:
            act = _gateup(h2, ffn_wg[i].astype(BF16), ffn_wu[i].astype(BF16))
            xs = _down(act, ffn_wd[i].astype(BF16), xs, ng(3), g_f)
    return xs.reshape(b, s, d)
```

```python
import functools
import math

import numpy as np
import jax
import jax.numpy as jnp
from jax import lax
from jax.experimental import pallas as pl
from jax.experimental.pallas import tpu as pltpu

F32 = jnp.float32
BF16 = jnp.bfloat16

CHUNK = 64
HEAD_DIM = 128
N_HEADS_A = 6
N_HEADS_B = 6
N_HEADS_C = 4
WIDTH_A = N_HEADS_A * HEAD_DIM
WIDTH_B = N_HEADS_B * HEAD_DIM
WIDTH_C = N_HEADS_C * HEAD_DIM
DIFF_DIM = HEAD_DIM // 2
LEFT_CHUNKS = 8
REL_CLIP = 256
ROPE_THETA = 500000.0
ROPE_DIM = DIFF_DIM // 4
N_EXPERTS = 8
TOP_K = 2
EPS = 1e-6
NEG_INF = -1e30
LOG2E = math.log2(math.e)

LANES = 128
VMEM_LIMIT = 56 * 1024 * 1024


def _cparams(sem):
    return pltpu.CompilerParams(dimension_semantics=sem, vmem_limit_bytes=VMEM_LIMIT)


def _tile(n, pref):
    t = min(n, pref)
    while n % t:
        t //= 2
    return t


def _rms(x, g):
    return (x * lax.rsqrt(jnp.mean(x * x, axis=-1, keepdims=True) + EPS)) * g


def _mods_kernel(c_ref, w_ref, b_ref, o_ref):
    c = c_ref[...]
    cond = c * jax.nn.sigmoid(c)
    o_ref[...] = jnp.sum(w_ref[...] * cond, axis=0, keepdims=True) + b_ref[...]


def _mods(c, mod_w, mod_b):
    depth, d, n = mod_w.shape
    tn = _tile(n, 1024)
    return pl.pallas_call(
        _mods_kernel,
        grid=(depth, n // tn),
        in_specs=[
            pl.BlockSpec((d, 1), lambda l, j: (0, 0)),
            pl.BlockSpec((None, d, tn), lambda l, j: (l, 0, j)),
            pl.BlockSpec((None, 1, tn), lambda l, j: (l, 0, j)),
        ],
        out_specs=pl.BlockSpec((None, 1, tn), lambda l, j: (l, 0, j)),
        out_shape=jax.ShapeDtypeStruct((depth, 1, n), F32),
        compiler_params=_cparams(("arbitrary", "arbitrary")),
        name="adaln_mods",
    )(c.reshape(d, 1), mod_w, mod_b.reshape(depth, 1, n))


ROPE_COL_LO = 3 * WIDTH_A + 3 * WIDTH_B
ROPE_COL_HI = ROPE_COL_LO + 2 * WIDTH_C
INPROJ_TN = 512


def _inproj_kernel(x_ref, g_ref, sc_ref, sh_ref, w_ref, cs_ref, wf_ref, bf_ref, pos_ref,
                   inv_ref, m1_ref, m2_ref, o_ref, lf_ref, h_scr, cos_scr, sin_scr):
    j = pl.program_id(1)

    @pl.when(j == 0)
    def _():
        h = _rms(x_ref[...], g_ref[...]) * (1.0 + sc_ref[...]) + sh_ref[...]
        hb = h.astype(BF16)
        h_scr[...] = hb
        fa = jnp.dot(hb, wf_ref[...], preferred_element_type=F32) + bf_ref[...]
        lf_ref[...] = jax.nn.log_sigmoid(fa)
        ang = pos_ref[...] * inv_ref[...]
        cos_scr[...] = jnp.cos(ang)
        sin_scr[...] = jnp.sin(ang)

    y = jnp.dot(h_scr[...], w_ref[...], preferred_element_type=F32) * cs_ref[...]
    is_rope = jnp.logical_and(j >= ROPE_COL_LO // INPROJ_TN, j < ROPE_COL_HI // INPROJ_TN)

    @pl.when(is_rope)
    def _():
        c = cos_scr[...]
        s = sin_scr[...]
        m1 = m1_ref[...]
        m2 = m2_ref[...]
        for hh in range(INPROJ_TN // LANES):
            yh = y[:, hh * LANES:(hh + 1) * LANES]
            up = pltpu.roll(yh, LANES - ROPE_DIM // 2, 1)
            dn = pltpu.roll(yh, ROPE_DIM // 2, 1)
            o_ref[:, hh * LANES:(hh + 1) * LANES] = (yh * c + s * (m1 * up + m2 * dn)).astype(BF16)

    @pl.when(jnp.logical_not(is_rope))
    def _():
        o_ref[...] = y.astype(BF16)


def _rope_lane_tables():
    lane = np.arange(LANES)
    r = lane % DIFF_DIM
    half = ROPE_DIM // 2
    inv = np.where(r < ROPE_DIM, ROPE_THETA ** (-((r % half) * 2.0 / ROPE_DIM)), 0.0)
    m1 = np.where(r < half, -1.0, 0.0)
    m2 = np.where((r >= half) & (r < ROPE_DIM), 1.0, 0.0)
    f = lambda a: jnp.asarray(a.reshape(1, LANES), F32)
    return f(inv), f(m1), f(m2)


def _inproj(x, g, sc, sh, w_main, colscale, w_f, b_f, pos_col):
    s, d = x.shape
    n = w_main.shape[1]
    tm = _tile(s, 1024)
    tn = INPROJ_TN
    inv, m1, m2 = _rope_lane_tables()
    row = lambda i, j: (i, 0)
    const = lambda i, j: (0, 0)
    return pl.pallas_call(
        _inproj_kernel,
        grid=(s // tm, n // tn),
        in_specs=[
            pl.BlockSpec((tm, d), row),
            pl.BlockSpec((1, d), const),
            pl.BlockSpec((1, d), const),
            pl.BlockSpec((1, d), const),
            pl.BlockSpec((d, tn), lambda i, j: (0, j)),
            pl.BlockSpec((1, tn), lambda i, j: (0, j)),
            pl.BlockSpec((d, LANES), const),
            pl.BlockSpec((1, LANES), const),
            pl.BlockSpec((tm, 1), row),
            pl.BlockSpec((1, LANES), const),
            pl.BlockSpec((1, LANES), const),
            pl.BlockSpec((1, LANES), const),
        ],
        out_specs=[
            pl.BlockSpec((tm, tn), lambda i, j: (i, j)),
            pl.BlockSpec((tm, LANES), row),
        ],
        out_shape=[
            jax.ShapeDtypeStruct((s, n), BF16),
            jax.ShapeDtypeStruct((s, LANES), F32),
        ],
        scratch_shapes=[
            pltpu.VMEM((tm, d), BF16),
            pltpu.VMEM((tm, LANES), F32),
            pltpu.VMEM((tm, LANES), F32),
        ],
        compiler_params=_cparams(("arbitrary", "arbitrary")),
        name="inproj",
    )(x, g, sc, sh, w_main, colscale, w_f, b_f, pos_col, inv, m1, m2)


DECAY_PIECES = 3


def _split3(x):
    x1 = x.astype(BF16)
    r1 = x - x1.astype(F32)
    x2 = r1.astype(BF16)
    return x1, x2, (r1 - x2.astype(F32)).astype(BF16)


def _cumsum_kernel(x_ref, o_ref, *rest, decay_heads):
    carry = rest[-1]

    @pl.when(pl.program_id(0) == 0)
    def _():
        carry[...] = jnp.zeros_like(carry)

    x = x_ref[...]
    tb = x.shape[0]
    r = lax.broadcasted_iota(jnp.int32, (tb, tb), 0)
    c = lax.broadcasted_iota(jnp.int32, (tb, tb), 1)
    tri = jnp.where(c <= r, 1.0, 0.0).astype(BF16)
    y = sum(jnp.dot(tri, piece, preferred_element_type=F32) for piece in _split3(x)) + carry[...]
    o_ref[...] = y
    carry[...] = y[tb - 1:tb, :]
    if decay_heads:
        dec_ref = rest[0]
        lane = lax.broadcasted_iota(jnp.int32, (tb, LANES), 1)
        for h in range(decay_heads):
            hi, mid, lo = [p.astype(F32) for p in _split3(-LOG2E * y[:, h:h + 1])]
            cols = jnp.where(lane == 0, hi, jnp.where(lane == 1, mid, jnp.where(lane == 2, lo, 0.0)))
            dec_ref[h] = cols.astype(BF16)


def _cumsum_rows(x, decay_heads=0):
    s = x.shape[0]
    tb = _tile(s, 256)
    out_specs = [pl.BlockSpec((tb, LANES), lambda i: (i, 0))]
    out_shape = [jax.ShapeDtypeStruct((s, LANES), F32)]
    if decay_heads:
        out_specs.append(pl.BlockSpec((decay_heads, tb, LANES), lambda i: (0, i, 0)))
        out_shape.append(jax.ShapeDtypeStruct((decay_heads, s, LANES), BF16))
    return pl.pallas_call(
        functools.partial(_cumsum_kernel, decay_heads=decay_heads),
        grid=(s // tb,),
        in_specs=[pl.BlockSpec((tb, LANES), lambda i: (i, 0))],
        out_specs=out_specs,
        out_shape=out_shape,
        scratch_shapes=[pltpu.VMEM((1, LANES), F32)],
        compiler_params=_cparams(("arbitrary",)),
        name="cumsum_rows",
    )(x)


def _qk(q, k):
    return lax.dot_general(q, k, (((1,), (1,)), ((), ())), preferred_element_type=F32)


def _ones_column(rows):
    lane = lax.broadcasted_iota(jnp.int32, (rows, LANES), 1)
    return jnp.where(lane == 0, 1.0, 0.0).astype(BF16)


def _online_update(s, v1, m, acc):
    m_new = jnp.maximum(m, jnp.max(s, axis=-1, keepdims=True))
    p = jnp.exp2(s - m_new)
    alpha = jnp.exp2(m - m_new)
    acc = alpha * acc + jnp.dot(p.astype(BF16), v1, preferred_element_type=F32)
    return m_new, acc


def _softmax_state(tq):
    return (jnp.full((tq, 1), NEG_INF, F32), jnp.zeros((tq, 2 * HEAD_DIM), F32))


def _normalize(acc):
    return acc[:, :HEAD_DIM] / acc[:, HEAD_DIM:HEAD_DIM + 1]


ATTN_TQ = 1024
ATTN_TK = 1024


def _attn_a_kernel(q_ref, k_ref, v_ref, fa_ref, g_ref, o_ref, *, tq, tk):
    qi = pl.program_id(1)
    lane = lax.broadcasted_iota(jnp.int32, (tq, LANES), 1)
    ones = jnp.where(lane < DECAY_PIECES, 1.0, 0.0).astype(BF16)
    q = jnp.concatenate([q_ref[...], ones], axis=1)

    def update(s, v, m, l, acc):
        m_new = jnp.maximum(m, jnp.max(s, axis=-1, keepdims=True))
        p = jnp.exp2(s - m_new)
        alpha = jnp.exp2(m - m_new)
        l = alpha * l + functools.reduce(jnp.add, [p[:, i:i + LANES] for i in range(0, tk, LANES)])
        return m_new, l, alpha * acc + jnp.dot(p.astype(BF16), v, preferred_element_type=F32)

    def block(kb, carry, masked):
        for j in range(tq // tk):
            start = pl.multiple_of(kb * tq + j * tk, tk)
            k = jnp.concatenate([k_ref[pl.ds(start, tk), :], fa_ref[pl.ds(start, tk), :]], axis=1)
            s = _qk(q, k)
            if masked:
                r = lax.broadcasted_iota(jnp.int32, (tq, tk), 0)
                c = lax.broadcasted_iota(jnp.int32, (tq, tk), 1) + j * tk
                s = jnp.where(c <= r, s, NEG_INF)
            carry = update(s, v_ref[pl.ds(start, tk), :], *carry)
        return carry

    init = (jnp.full((tq, 1), NEG_INF, F32), jnp.zeros((tq, LANES), F32), jnp.zeros((tq, HEAD_DIM), F32))
    carry = lax.fori_loop(0, qi, lambda kb, cr: block(kb, cr, False), init)
    _, l, acc = block(qi, carry, True)
    o_ref[...] = _rms(acc / jnp.sum(l, axis=-1, keepdims=True), g_ref[...]).astype(BF16)


def _attn_a(proj, decay, g):
    s = proj.shape[0]
    tq = _tile(s, ATTN_TQ)
    nh = N_HEADS_A
    kern = functools.partial(_attn_a_kernel, tq=tq, tk=_tile(tq, ATTN_TK))
    return pl.pallas_call(
        kern,
        grid=(nh, s // tq),
        in_specs=[
            pl.BlockSpec((tq, HEAD_DIM), lambda h, i: (i, h)),
            pl.BlockSpec((s, HEAD_DIM), lambda h, i: (0, nh + h)),
            pl.BlockSpec((s, HEAD_DIM), lambda h, i: (0, 2 * nh + h)),
            pl.BlockSpec((None, s, LANES), lambda h, i: (h, 0, 0)),
            pl.BlockSpec((1, HEAD_DIM), lambda h, i: (0, 0)),
        ],
        out_specs=pl.BlockSpec((tq, HEAD_DIM), lambda h, i: (i, h)),
        out_shape=jax.ShapeDtypeStruct((s, WIDTH_A), BF16),
        compiler_params=_cparams(("arbitrary", "arbitrary")),
        name="attn_forget",
    )(proj, proj, proj, decay, g)


TQB = 256
BAND = LEFT_CHUNKS * CHUNK
NWIN = (BAND + TQB) // TQB
BAND_STEP_ROWS = 1024


def _attn_b_kernel(q_ref, k_ref, v_ref, bias_ref, g_ref, o_ref, *, tiles):
    ones_col = _ones_column(TQB)
    for t in range(tiles):
        qt = pl.program_id(1) * tiles + t
        q = q_ref[t * TQB:(t + 1) * TQB, :]
        starts, ss = [], []
        for w in range(NWIN):
            kb = qt - (NWIN - 1) + w
            starts.append(pl.multiple_of(jnp.maximum(kb, 0) * TQB, TQB))
            sw = _qk(q, k_ref[pl.ds(starts[w], TQB), :]) + bias_ref[:, w * TQB:(w + 1) * TQB]
            ss.append(jnp.where(kb >= 0, sw, NEG_INF))
        m = functools.reduce(jnp.maximum, [jnp.max(sw, axis=-1, keepdims=True) for sw in ss])
        acc = jnp.zeros((TQB, 2 * HEAD_DIM), F32)
        for w in range(NWIN):
            v1 = jnp.concatenate([v_ref[pl.ds(starts[w], TQB), :], ones_col], axis=1)
            acc = acc + jnp.dot(jnp.exp2(ss[w] - m).astype(BF16), v1, preferred_element_type=F32)
        o_ref[t * TQB:(t + 1) * TQB, :] = _rms(_normalize(acc), g_ref[...]).astype(BF16)


def _band_bias_table(rel_bias):
    w = NWIN * TQB
    n = TQB + w
    u = np.arange(n) - (TQB - 1)
    dist = (NWIN - 1) * TQB - u
    idx = np.clip(dist, -REL_CLIP, REL_CLIP) + REL_CLIP
    diag = rel_bias.astype(F32)[:, idx] * LOG2E
    diag = jnp.roll(diag, -(TQB - 1), axis=1)
    nh = rel_bias.shape[0]
    skew = jnp.tile(diag, (1, TQB))[:, :TQB * (n - 1)].reshape(nh, TQB, n - 1)[:, :, :w]
    qpos = np.arange(TQB)[:, None]
    kpos = np.arange(w)[None, :] - (NWIN - 1) * TQB
    qc = qpos // CHUNK
    kc = np.floor_divide(kpos, CHUNK)
    valid = (kc <= qc) & (kc >= qc - LEFT_CHUNKS)
    return jnp.where(jnp.asarray(valid)[None], skew, NEG_INF)


def _attn_b(proj, bias_tbl, g):
    s = proj.shape[0]
    nh = N_HEADS_B
    c0 = 3 * N_HEADS_A
    tq = _tile(s, BAND_STEP_ROWS)
    return pl.pallas_call(
        functools.partial(_attn_b_kernel, tiles=tq // TQB),
        grid=(nh, s // tq),
        in_specs=[
            pl.BlockSpec((tq, HEAD_DIM), lambda h, i: (i, c0 + h)),
            pl.BlockSpec((s, HEAD_DIM), lambda h, i: (0, c0 + nh + h)),
            pl.BlockSpec((s, HEAD_DIM), lambda h, i: (0, c0 + 2 * nh + h)),
            pl.BlockSpec((None, TQB, NWIN * TQB), lambda h, i: (h, 0, 0)),
            pl.BlockSpec((1, HEAD_DIM), lambda h, i: (0, 0)),
        ],
        out_specs=pl.BlockSpec((tq, HEAD_DIM), lambda h, i: (i, h)),
        out_shape=jax.ShapeDtypeStruct((s, WIDTH_B), BF16),
        compiler_params=_cparams(("arbitrary", "arbitrary")),
        name="attn_band",
    )(proj, proj, proj, bias_tbl, g)


def _attn_c_kernel(q_ref, k_ref, v_ref, lam_ref, g_ref, o_ref, *, tq, tk, lam_init):
    qi = pl.program_id(1)
    q = q_ref[...]
    lane = lax.broadcasted_iota(jnp.int32, (1, HEAD_DIM), 1)
    zero = jnp.zeros_like(q)
    q0 = jnp.where(lane < DIFF_DIM, q, zero)
    q1 = jnp.where(lane >= DIFF_DIM, q, zero)
    ones_col = _ones_column(tk)

    def block(kb, carry, masked):
        st0, st1 = carry[:2], carry[2:]
        for j in range(tq // tk):
            start = pl.multiple_of(kb * tq + j * tk, tk)
            k = k_ref[pl.ds(start, tk), :]
            v = jnp.concatenate([v_ref[pl.ds(start, tk), :], ones_col], axis=1)
            s0 = _qk(q0, k)
            s1 = _qk(q1, k)
            if masked:
                r = lax.broadcasted_iota(jnp.int32, (tq, tk), 0) // CHUNK
                c = (lax.broadcasted_iota(jnp.int32, (tq, tk), 1) + j * tk) // CHUNK
                ok = c <= r
                s0 = jnp.where(ok, s0, NEG_INF)
                s1 = jnp.where(ok, s1, NEG_INF)
            st0 = _online_update(s0, v, *st0)
            st1 = _online_update(s1, v, *st1)
        return st0 + st1

    st = _softmax_state(tq)
    carry = lax.fori_loop(0, qi, lambda kb, cr: block(kb, cr, False), st + st)
    _, a0, _, a1 = block(qi, carry, True)
    lam = lam_ref[...]
    lam_full = (jnp.exp(jnp.sum(lam[0:1] * lam[1:2], axis=-1, keepdims=True))
                - jnp.exp(jnp.sum(lam[2:3] * lam[3:4], axis=-1, keepdims=True)) + lam_init)
    o = _normalize(a0) - lam_full * _normalize(a1)
    o_ref[...] = (_rms(o, g_ref[...]) * (1.0 - lam_init)).astype(BF16)


def _attn_c(proj, lam, g, lam_init):
    s = proj.shape[0]
    tq = _tile(s, ATTN_TQ)
    nh = N_HEADS_C
    c0 = 3 * N_HEADS_A + 3 * N_HEADS_B
    kern = functools.partial(_attn_c_kernel, tq=tq, tk=_tile(tq, ATTN_TK), lam_init=lam_init)
    return pl.pallas_call(
        kern,
        grid=(nh, s // tq),
        in_specs=[
            pl.BlockSpec((tq, HEAD_DIM), lambda h, i: (i, c0 + h)),
            pl.BlockSpec((s, HEAD_DIM), lambda h, i: (0, c0 + nh + h)),
            pl.BlockSpec((s, HEAD_DIM), lambda h, i: (0, c0 + 2 * nh + h)),
            pl.BlockSpec((4, DIFF_DIM), lambda h, i: (0, 0)),
            pl.BlockSpec((1, HEAD_DIM), lambda h, i: (0, 0)),
        ],
        out_specs=pl.BlockSpec((tq, HEAD_DIM), lambda h, i: (i, h)),
        out_shape=jax.ShapeDtypeStruct((s, WIDTH_C), BF16),
        compiler_params=_cparams(("arbitrary", "arbitrary")),
        name="attn_diff",
    )(proj, proj, proj, lam, g)


def _residual(x, y, g_post, gate):
    return x + gate * _rms(y, g_post)


def _outproj_kernel(oa_ref, ob_ref, oc_ref, wa_ref, wb_ref, wc_ref, x_ref, gp_ref, gate_ref,
                    gn_ref, sc_ref, sh_ref, xo_ref, ho_ref):
    y = (jnp.dot(oa_ref[...], wa_ref[...], preferred_element_type=F32)
         + jnp.dot(ob_ref[...], wb_ref[...], preferred_element_type=F32)
         + jnp.dot(oc_ref[...], wc_ref[...], preferred_element_type=F32))
    xn = _residual(x_ref[...], y, gp_ref[...], gate_ref[...])
    xo_ref[...] = xn
    ho_ref[...] = (_rms(xn, gn_ref[...]) * (1.0 + sc_ref[...]) + sh_ref[...]).astype(ho_ref.dtype)


def _outproj(oa, ob, oc, w_o, x, g_post, gate, g_next, sc, sh, h_dtype):
    s, d = x.shape
    tm = _tile(s, 256)
    row = lambda i: (i, 0)
    const = lambda i: (0, 0)
    assert WIDTH_A == WIDTH_B and (WIDTH_A + WIDTH_B) % WIDTH_C == 0
    return pl.pallas_call(
        _outproj_kernel,
        grid=(s // tm,),
        in_specs=[
            pl.BlockSpec((tm, WIDTH_A), row),
            pl.BlockSpec((tm, WIDTH_B), row),
            pl.BlockSpec((tm, WIDTH_C), row),
            pl.BlockSpec((WIDTH_A, d), lambda i: (0, 0)),
            pl.BlockSpec((WIDTH_B, d), lambda i: (1, 0)),
            pl.BlockSpec((WIDTH_C, d), lambda i: ((WIDTH_A + WIDTH_B) // WIDTH_C, 0)),
            pl.BlockSpec((tm, d), row),
            pl.BlockSpec((1, d), const),
            pl.BlockSpec((1, d), const),
            pl.BlockSpec((1, d), const),
            pl.BlockSpec((1, d), const),
            pl.BlockSpec((1, d), const),
        ],
        out_specs=[pl.BlockSpec((tm, d), row), pl.BlockSpec((tm, d), row)],
        out_shape=[jax.ShapeDtypeStruct((s, d), F32), jax.ShapeDtypeStruct((s, d), h_dtype)],
        compiler_params=_cparams(("arbitrary",)),
        name="outproj_residual",
    )(oa, ob, oc, w_o, w_o, w_o, x, g_post, gate, g_next, sc, sh)


def _gateup_kernel(h_ref, wg_ref, wu_ref, o_ref):
    h = h_ref[...]
    a = jnp.dot(h, wg_ref[...], preferred_element_type=F32)
    u = jnp.dot(h, wu_ref[...], preferred_element_type=F32)
    o_ref[...] = ((a * jax.nn.sigmoid(a)) * u).astype(BF16)


def _gateup(h, wg, wu):
    s, d = h.shape
    f = wg.shape[1]
    tm = _tile(s, 1024)
    tf = _tile(f, 512)
    return pl.pallas_call(
        _gateup_kernel,
        grid=(s // tm, f // tf),
        in_specs=[
            pl.BlockSpec((tm, d), lambda i, j: (i, 0)),
            pl.BlockSpec((d, tf), lambda i, j: (0, j)),
            pl.BlockSpec((d, tf), lambda i, j: (0, j)),
        ],
        out_specs=pl.BlockSpec((tm, tf), lambda i, j: (i, j)),
        out_shape=jax.ShapeDtypeStruct((s, f), BF16),
        compiler_params=_cparams(("arbitrary", "arbitrary")),
        name="ffn_gateup",
    )(h, wg, wu)


def _down_kernel(a_ref, w_ref, x_ref, gp_ref, gate_ref, o_ref):
    y = jnp.dot(a_ref[...], w_ref[...], preferred_element_type=F32)
    o_ref[...] = _residual(x_ref[...], y, gp_ref[...], gate_ref[...])


def _down(act, wd, x, g_post, gate):
    s, f = act.shape
    d = wd.shape[1]
    tm = _tile(s, 256)
    return pl.pallas_call(
        _down_kernel,
        grid=(s // tm,),
        in_specs=[
            pl.BlockSpec((tm, f), lambda i: (i, 0)),
            pl.BlockSpec((f, d), lambda i: (0, 0), pipeline_mode=pl.Buffered(1)),
            pl.BlockSpec((tm, d), lambda i: (i, 0)),
            pl.BlockSpec((1, d), lambda i: (0, 0)),
            pl.BlockSpec((1, d), lambda i: (0, 0)),
        ],
        out_specs=pl.BlockSpec((tm, d), lambda i: (i, 0)),
        out_shape=jax.ShapeDtypeStruct((s, d), F32),
        compiler_params=_cparams(("arbitrary",)),
        name="ffn_down_residual",
    )(act, wd, x, g_post, gate)


def _split2(a):
    hi = a.astype(BF16)
    return hi, (a - hi.astype(F32)).astype(BF16)


def _router_kernel(h_ref, w_ref, b_ref, route_ref, mh_ref):
    hh, hl = _split2(h_ref[...])
    wh, wl = _split2(w_ref[...])
    logits = (jnp.dot(hh, wh, preferred_element_type=F32)
              + jnp.dot(hh, wl, preferred_element_type=F32)
              + jnp.dot(hl, wh, preferred_element_type=F32)) + b_ref[...]
    lane = lax.broadcasted_iota(jnp.int32, logits.shape, 1)
    logits = jnp.where(lane < N_EXPERTS, logits, NEG_INF)
    v1 = jnp.max(logits, axis=-1, keepdims=True)
    i1 = jnp.min(jnp.where(logits == v1, lane, LANES), axis=-1, keepdims=True)
    rest = jnp.where(lane == i1, NEG_INF, logits)
    v2 = jnp.max(rest, axis=-1, keepdims=True)
    i2 = jnp.min(jnp.where(rest == v2, lane, LANES), axis=-1, keepdims=True)
    e = jnp.exp(v2 - v1)
    g1 = 1.0 / (1.0 + e)
    g2 = e / (1.0 + e)
    route_ref[...] = jnp.where(lane == 0, i1.astype(F32),
                               jnp.where(lane == 1, i2.astype(F32),
                                         jnp.where(lane == 2, g1, jnp.where(lane == 3, g2, 0.0))))
    mh_ref[...] = jnp.where(jnp.logical_or(lane == i1, lane == i2), 1.0, 0.0)


def _router(h, rw_pad, rb_pad):
    s, d = h.shape
    tm = _tile(s, 512)
    return pl.pallas_call(
        _router_kernel,
        grid=(s // tm,),
        in_specs=[
            pl.BlockSpec((tm, d), lambda i: (i, 0)),
            pl.BlockSpec((d, LANES), lambda i: (0, 0)),
            pl.BlockSpec((1, LANES), lambda i: (0, 0)),
        ],
        out_specs=[pl.BlockSpec((tm, LANES), lambda i: (i, 0)),
                   pl.BlockSpec((tm, LANES), lambda i: (i, 0))],
        out_shape=[jax.ShapeDtypeStruct((s, LANES), F32), jax.ShapeDtypeStruct((s, LANES), F32)],
        compiler_params=_cparams(("arbitrary",)),
        name="moe_router",
    )(h, rw_pad, rb_pad)


def _dispatch_kernel(pos_ref, h_ref, init_ref, o_ref, sem):
    del init_ref
    n = h_ref.shape[0]

    def copy(r, k):
        return pltpu.make_async_copy(h_ref.at[pl.ds(r, 1)], o_ref.at[pl.ds(pos_ref[0, 0, 2 * r + k], 1)], sem)

    def issue(r, c):
        copy(r, 0).start()
        copy(r, 1).start()
        return c

    def drain(r, c):
        copy(r, 0).wait()
        copy(r, 1).wait()
        return c

    lax.fori_loop(0, n, issue, 0)
    lax.fori_loop(0, n, drain, 0)


def _dispatch(h, pos, n_slots):
    s, d = h.shape
    tm = _tile(s, 256)
    init = jnp.zeros((n_slots, d), F32)
    return pl.pallas_call(
        _dispatch_kernel,
        grid=(s // tm,),
        in_specs=[
            pl.BlockSpec((1, 1, TOP_K * tm), lambda i: (i, 0, 0), memory_space=pltpu.SMEM),
            pl.BlockSpec((tm, d), lambda i: (i, 0)),
            pl.BlockSpec(memory_space=pl.ANY),
        ],
        out_specs=pl.BlockSpec(memory_space=pl.ANY),
        out_shape=jax.ShapeDtypeStruct((n_slots, d), F32),
        scratch_shapes=[pltpu.SemaphoreType.DMA(())],
        input_output_aliases={2: 0},
        compiler_params=_cparams(("arbitrary",)),
        name="moe_dispatch",
    )(pos.reshape(s // tm, 1, TOP_K * tm), h, init)


def _ggateup_kernel(te_ref, nu_ref, h_ref, wg_ref, wu_ref, o_ref, wg_scr, wu_scr):
    i = pl.program_id(1)
    used = i < nu_ref[0]
    new_weights = jnp.logical_or(i == 0, te_ref[i] != te_ref[jnp.maximum(i - 1, 0)])

    @pl.when(jnp.logical_and(used, new_weights))
    def _():
        wg_scr[...] = wg_ref[...].astype(BF16)
        wu_scr[...] = wu_ref[...].astype(BF16)

    @pl.when(used)
    def _():
        h = h_ref[...].astype(BF16)
        a = jnp.dot(h, wg_scr[...], preferred_element_type=F32)
        u = jnp.dot(h, wu_scr[...], preferred_element_type=F32)
        o_ref[...] = ((a * jax.nn.sigmoid(a)) * u).astype(BF16)

    @pl.when(jnp.logical_not(used))
    def _():
        o_ref[...] = jnp.zeros_like(o_ref)


def _ggateup(hs, wg, wu, tile_expert, n_used, tm):
    p, d = hs.shape
    f = wg.shape[2]
    tf = _tile(f, 512)
    return pl.pallas_call(
        _ggateup_kernel,
        grid_spec=pltpu.PrefetchScalarGridSpec(
            num_scalar_prefetch=2,
            grid=(f // tf, p // tm),
            in_specs=[
                pl.BlockSpec((tm, d), lambda j, i, te, nu: (i, 0)),
                pl.BlockSpec((None, d, tf), lambda j, i, te, nu: (te[i], 0, j)),
                pl.BlockSpec((None, d, tf), lambda j, i, te, nu: (te[i], 0, j)),
            ],
            out_specs=pl.BlockSpec((tm, tf), lambda j, i, te, nu: (i, j)),
            scratch_shapes=[pltpu.VMEM((d, tf), BF16), pltpu.VMEM((d, tf), BF16)],
        ),
        out_shape=jax.ShapeDtypeStruct((p, f), BF16),
        compiler_params=_cparams(("arbitrary", "arbitrary")),
        name="moe_gateup",
    )(tile_expert, n_used, hs, wg, wu)


def _gdown_kernel(te_ref, nu_ref, a_ref, w_ref, o_ref):
    used = pl.program_id(1) < nu_ref[0]

    @pl.when(used)
    def _():
        o_ref[...] = jnp.dot(a_ref[...], w_ref[...], preferred_element_type=F32)

    @pl.when(jnp.logical_not(used))
    def _():
        o_ref[...] = jnp.zeros_like(o_ref)


def _gdown(act, wd, tile_expert, n_used, tm):
    p, f = act.shape
    d = wd.shape[2]
    tn = _tile(d, 1024)
    return pl.pallas_call(
        _gdown_kernel,
        grid_spec=pltpu.PrefetchScalarGridSpec(
            num_scalar_prefetch=2,
            grid=(d // tn, p // tm),
            in_specs=[
                pl.BlockSpec((tm, f), lambda n, i, te, nu: (i, 0)),
                pl.BlockSpec((None, f, tn), lambda n, i, te, nu: (te[i], 0, n)),
            ],
            out_specs=pl.BlockSpec((tm, tn), lambda n, i, te, nu: (i, n)),
        ),
        out_shape=jax.ShapeDtypeStruct((p, d), F32),
        compiler_params=_cparams(("arbitrary", "arbitrary")),
        name="moe_down",
    )(tile_expert, n_used, act, wd)


def _combine_kernel(pos_ref, y_ref, route_ref, x_ref, gp_ref, gate_ref, o_ref, buf0, buf1, sem):
    n = x_ref.shape[0]

    def copy(r, k, buf):
        return pltpu.make_async_copy(y_ref.at[pl.ds(pos_ref[0, 0, 2 * r + k], 1)], buf.at[pl.ds(r, 1)], sem)

    def issue(r, c):
        copy(r, 0, buf0).start()
        copy(r, 1, buf1).start()
        return c

    def drain(r, c):
        copy(r, 0, buf0).wait()
        copy(r, 1, buf1).wait()
        return c

    lax.fori_loop(0, n, issue, 0)
    lax.fori_loop(0, n, drain, 0)
    route = route_ref[...]
    y = route[:, 2:3] * buf0[...] + route[:, 3:4] * buf1[...]
    o_ref[...] = _residual(x_ref[...], y, gp_ref[...], gate_ref[...])


def _combine(ys, pos, route, x, g_post, gate):
    s, d = x.shape
    tm = _tile(s, 256)
    return pl.pallas_call(
        _combine_kernel,
        grid=(s // tm,),
        in_specs=[
            pl.BlockSpec((1, 1, TOP_K * tm), lambda i: (i, 0, 0), memory_space=pltpu.SMEM),
            pl.BlockSpec(memory_space=pl.ANY),
            pl.BlockSpec((tm, LANES), lambda i: (i, 0)),
            pl.BlockSpec((tm, d), lambda i: (i, 0)),
            pl.BlockSpec((1, d), lambda i: (0, 0)),
            pl.BlockSpec((1, d), lambda i: (0, 0)),
        ],
        out_specs=pl.BlockSpec((tm, d), lambda i: (i, 0)),
        out_shape=jax.ShapeDtypeStruct((s, d), F32),
        scratch_shapes=[pltpu.VMEM((tm, d), F32), pltpu.VMEM((tm, d), F32), pltpu.SemaphoreType.DMA(())],
        compiler_params=_cparams(("arbitrary",)),
        name="moe_combine_residual",
    )(pos.reshape(s // tm, 1, TOP_K * tm), ys, route, x, g_post, gate)


MOE_TM = 512


def _moe(h, x, router_w, router_b, wg, wu, wd, g_post, gate):
    s, d = h.shape
    pad = lambda a: jnp.pad(a.astype(F32), ((0, 0), (0, LANES - N_EXPERTS)))
    route, mh = _router(h, pad(router_w), pad(router_b.reshape(1, N_EXPERTS)))
    csum, = _cumsum_rows(mh)
    tm = _tile(TOP_K * s, MOE_TM)
    counts = csum[s - 1, :N_EXPERTS].astype(jnp.int32)
    padded = ((counts + tm - 1) // tm) * tm
    ends = jnp.cumsum(padded)
    starts = ends - padded
    ids = route[:, :TOP_K].astype(jnp.int32)
    slot_of = starts[None, :] + (csum[:, :N_EXPERTS] - mh[:, :N_EXPERTS]).astype(jnp.int32)
    pos = jnp.take_along_axis(slot_of, ids, axis=1)
    n_tiles = (TOP_K * s) // tm + N_EXPERTS
    n_used = (ends[N_EXPERTS - 1] // tm).astype(jnp.int32).reshape(1)
    tile_start = jnp.minimum(jnp.arange(n_tiles, dtype=jnp.int32), n_used[0] - 1) * tm
    tile_expert = jnp.sum(tile_start[:, None] >= ends[None, :], axis=1).astype(jnp.int32)
    hs = _dispatch(h, pos, n_tiles * tm)
    act = _ggateup(hs, wg, wu, tile_expert, n_used, tm)
    ys = _gdown(act, wd, tile_expert, n_used, tm)
    return _combine(ys, pos, route, x, g_post, gate)


def kernel(x, c, positions, mod_w, mod_b, norm_g, w_in, b_f, rel_bias, lam, onorm, w_o,
           ffn_wg, ffn_wu, ffn_wd, router_w, router_b, exp_wg, exp_wu, exp_wd):
    b, s, d = x.shape
    assert b == 1 and s % TQB == 0
    depth = mod_w.shape[0]
    xs = x.reshape(s, d)
    pos_col = positions.reshape(s, 1).astype(F32)
    mods = _mods(c, mod_w, mod_b)

    f0 = 3 * WIDTH_A
    scale_ab = HEAD_DIM ** -0.5 * LOG2E
    scale_c = DIFF_DIM ** -0.5 * LOG2E
    colscale = np.ones((1, w_in.shape[2] - N_HEADS_A), np.float32)
    colscale[0, 0:WIDTH_A] = scale_ab
    colscale[0, f0:f0 + WIDTH_B] = scale_ab
    colscale[0, ROPE_COL_LO:ROPE_COL_LO + WIDTH_C] = scale_c
    colscale = jnp.asarray(colscale)

    for l in range(depth):
        sh_m, sc_m, g_m, sh_f, sc_f, g_f = [mods[l, :, i * d:(i + 1) * d] for i in range(6)]
        ng = lambda i: norm_g[l, i].reshape(1, d).astype(F32)
        w_main = jnp.concatenate([w_in[l, :, :f0], w_in[l, :, f0 + N_HEADS_A:]], axis=1).astype(BF16)
        w_f = jnp.pad(w_in[l, :, f0:f0 + N_HEADS_A], ((0, 0), (0, LANES - N_HEADS_A))).astype(BF16)
        bf = jnp.pad(b_f[l].astype(F32).reshape(1, N_HEADS_A), ((0, 0), (0, LANES - N_HEADS_A)))
        proj, logf = _inproj(xs, ng(0), sc_m, sh_m, w_main, colscale, w_f, bf, pos_col)
        _, decay = _cumsum_rows(logf, decay_heads=N_HEADS_A)
        on = lambda i: onorm[l, i].reshape(1, HEAD_DIM).astype(F32)
        lam_init = 0.8 - 0.6 * math.exp(-0.3 * l)
        oa = _attn_a(proj, decay, on(0))
        ob = _attn_b(proj, _band_bias_table(rel_bias[l]), on(1))
        oc = _attn_c(proj, lam[l].astype(F32), on(2), lam_init)
        moe = l % 2 == 1
        xs, h2 = _outproj(oa, ob, oc, w_o[l].astype(BF16), xs, ng(1), g_m, ng(2), sc_f, sh_f,
                          F32 if moe else BF16)
        i = l // 2
        if moe:
            xs = _moe(h2, xs, router_w[i], router_b[i], exp_wg[i], exp_wu[i], exp_wd[i].astype(BF16), ng(3), g_f)
        else:
            act = _gateup(h2, ffn_wg[i].astype(BF16), ffn_wu[i].astype(BF16))
            xs = _down(act, ffn_wd[i].astype(BF16), xs, ng(3), g_f)
    return xs.reshape(b, s, d)
```

```python
import functools
import math

import numpy as np
import jax
import jax.numpy as jnp
from jax import lax
from jax.experimental import pallas as pl
from jax.experimental.pallas import tpu as pltpu

F32 = jnp.float32
BF16 = jnp.bfloat16

CHUNK = 64
HEAD_DIM = 128
N_HEADS_A = 6
N_HEADS_B = 6
N_HEADS_C = 4
WIDTH_A = N_HEADS_A * HEAD_DIM
WIDTH_B = N_HEADS_B * HEAD_DIM
WIDTH_C = N_HEADS_C * HEAD_DIM
DIFF_DIM = HEAD_DIM // 2
LEFT_CHUNKS = 8
REL_CLIP = 256
ROPE_THETA = 500000.0
ROPE_DIM = DIFF_DIM // 4
N_EXPERTS = 8
TOP_K = 2
EPS = 1e-6
NEG_INF = -1e30
LOG2E = math.log2(math.e)

LANES = 128
VMEM_LIMIT = 56 * 1024 * 1024


def _cparams(sem):
    return pltpu.CompilerParams(dimension_semantics=sem, vmem_limit_bytes=VMEM_LIMIT)


def _tile(n, pref):
    t = min(n, pref)
    while n % t:
        t //= 2
    return t


def _rms(x, g):
    return (x * lax.rsqrt(jnp.mean(x * x, axis=-1, keepdims=True) + EPS)) * g


def _mods_kernel(c_ref, w_ref, b_ref, o_ref):
    c = c_ref[...]
    cond = c * jax.nn.sigmoid(c)
    o_ref[...] = jnp.sum(w_ref[...] * cond, axis=0, keepdims=True) + b_ref[...]


def _mods(c, mod_w, mod_b):
    depth, d, n = mod_w.shape
    tn = _tile(n, 1024)
    return pl.pallas_call(
        _mods_kernel,
        grid=(depth, n // tn),
        in_specs=[
            pl.BlockSpec((d, 1), lambda l, j: (0, 0)),
            pl.BlockSpec((None, d, tn), lambda l, j: (l, 0, j)),
            pl.BlockSpec((None, 1, tn), lambda l, j: (l, 0, j)),
        ],
        out_specs=pl.BlockSpec((None, 1, tn), lambda l, j: (l, 0, j)),
        out_shape=jax.ShapeDtypeStruct((depth, 1, n), F32),
        compiler_params=_cparams(("arbitrary", "arbitrary")),
        name="adaln_mods",
    )(c.reshape(d, 1), mod_w, mod_b.reshape(depth, 1, n))


ROPE_COL_LO = 3 * WIDTH_A + 3 * WIDTH_B
ROPE_COL_HI = ROPE_COL_LO + 2 * WIDTH_C
INPROJ_TN = 512


def _inproj_kernel(x_ref, g_ref, sc_ref, sh_ref, w_ref, cs_ref, wf_ref, bf_ref, pos_ref,
                   inv_ref, m1_ref, m2_ref, o_ref, lf_ref, h_scr, cos_scr, sin_scr):
    j = pl.program_id(1)

    @pl.when(j == 0)
    def _():
        h = _rms(x_ref[...], g_ref[...]) * (1.0 + sc_ref[...]) + sh_ref[...]
        hb = h.astype(BF16)
        h_scr[...] = hb
        fa = jnp.dot(hb, wf_ref[...], preferred_element_type=F32) + bf_ref[...]
        lf_ref[...] = jax.nn.log_sigmoid(fa)
        ang = pos_ref[...] * inv_ref[...]
        cos_scr[...] = jnp.cos(ang)
        sin_scr[...] = jnp.sin(ang)

    y = jnp.dot(h_scr[...], w_ref[...], preferred_element_type=F32) * cs_ref[...]
    is_rope = jnp.logical_and(j >= ROPE_COL_LO // INPROJ_TN, j < ROPE_COL_HI // INPROJ_TN)

    @pl.when(is_rope)
    def _():
        c = cos_scr[...]
        s = sin_scr[...]
        m1 = m1_ref[...]
        m2 = m2_ref[...]
        for hh in range(INPROJ_TN // LANES):
            yh = y[:, hh * LANES:(hh + 1) * LANES]
            up = pltpu.roll(yh, LANES - ROPE_DIM // 2, 1)
            dn = pltpu.roll(yh, ROPE_DIM // 2, 1)
            o_ref[:, hh * LANES:(hh + 1) * LANES] = (yh * c + s * (m1 * up + m2 * dn)).astype(BF16)

    @pl.when(jnp.logical_not(is_rope))
    def _():
        o_ref[...] = y.astype(BF16)


def _rope_lane_tables():
    lane = np.arange(LANES)
    r = lane % DIFF_DIM
    half = ROPE_DIM // 2
    inv = np.where(r < ROPE_DIM, ROPE_THETA ** (-((r % half) * 2.0 / ROPE_DIM)), 0.0)
    m1 = np.where(r < half, -1.0, 0.0)
    m2 = np.where((r >= half) & (r < ROPE_DIM), 1.0, 0.0)
    f = lambda a: jnp.asarray(a.reshape(1, LANES), F32)
    return f(inv), f(m1), f(m2)


def _inproj(x, g, sc, sh, w_main, colscale, w_f, b_f, pos_col):
    s, d = x.shape
    n = w_main.shape[1]
    tm = _tile(s, 1024)
    tn = INPROJ_TN
    inv, m1, m2 = _rope_lane_tables()
    row = lambda i, j: (i, 0)
    const = lambda i, j: (0, 0)
    return pl.pallas_call(
        _inproj_kernel,
        grid=(s // tm, n // tn),
        in_specs=[
            pl.BlockSpec((tm, d), row),
            pl.BlockSpec((1, d), const),
            pl.BlockSpec((1, d), const),
            pl.BlockSpec((1, d), const),
            pl.BlockSpec((d, tn), lambda i, j: (0, j)),
            pl.BlockSpec((1, tn), lambda i, j: (0, j)),
            pl.BlockSpec((d, LANES), const),
            pl.BlockSpec((1, LANES), const),
            pl.BlockSpec((tm, 1), row),
            pl.BlockSpec((1, LANES), const),
            pl.BlockSpec((1, LANES), const),
            pl.BlockSpec((1, LANES), const),
        ],
        out_specs=[
            pl.BlockSpec((tm, tn), lambda i, j: (i, j)),
            pl.BlockSpec((tm, LANES), row),
        ],
        out_shape=[
            jax.ShapeDtypeStruct((s, n), BF16),
            jax.ShapeDtypeStruct((s, LANES), F32),
        ],
        scratch_shapes=[
            pltpu.VMEM((tm, d), BF16),
            pltpu.VMEM((tm, LANES), F32),
            pltpu.VMEM((tm, LANES), F32),
        ],
        compiler_params=_cparams(("arbitrary", "arbitrary")),
        name="inproj",
    )(x, g, sc, sh, w_main, colscale, w_f, b_f, pos_col, inv, m1, m2)


DECAY_PIECES = 3


def _split3(x):
    x1 = x.astype(BF16)
    r1 = x - x1.astype(F32)
    x2 = r1.astype(BF16)
    return x1, x2, (r1 - x2.astype(F32)).astype(BF16)


def _cumsum_kernel(x_ref, o_ref, *rest, decay_heads):
    carry = rest[-1]

    @pl.when(pl.program_id(0) == 0)
    def _():
        carry[...] = jnp.zeros_like(carry)

    x = x_ref[...]
    tb = x.shape[0]
    r = lax.broadcasted_iota(jnp.int32, (tb, tb), 0)
    c = lax.broadcasted_iota(jnp.int32, (tb, tb), 1)
    tri = jnp.where(c <= r, 1.0, 0.0).astype(BF16)
    y = sum(jnp.dot(tri, piece, preferred_element_type=F32) for piece in _split3(x)) + carry[...]
    o_ref[...] = y
    carry[...] = y[tb - 1:tb, :]
    if decay_heads:
        dec_ref = rest[0]
        lane = lax.broadcasted_iota(jnp.int32, (tb, LANES), 1)
        for h in range(decay_heads):
            hi, mid, lo = [p.astype(F32) for p in _split3(-LOG2E * y[:, h:h + 1])]
            cols = jnp.where(lane == 0, hi, jnp.where(lane == 1, mid, jnp.where(lane == 2, lo, 0.0)))
            dec_ref[h] = cols.astype(BF16)


def _cumsum_rows(x, decay_heads=0):
    s = x.shape[0]
    tb = _tile(s, 256)
    out_specs = [pl.BlockSpec((tb, LANES), lambda i: (i, 0))]
    out_shape = [jax.ShapeDtypeStruct((s, LANES), F32)]
    if decay_heads:
        out_specs.append(pl.BlockSpec((decay_heads, tb, LANES), lambda i: (0, i, 0)))
        out_shape.append(jax.ShapeDtypeStruct((decay_heads, s, LANES), BF16))
    return pl.pallas_call(
        functools.partial(_cumsum_kernel, decay_heads=decay_heads),
        grid=(s // tb,),
        in_specs=[pl.BlockSpec((tb, LANES), lambda i: (i, 0))],
        out_specs=out_specs,
        out_shape=out_shape,
        scratch_shapes=[pltpu.VMEM((1, LANES), F32)],
        compiler_params=_cparams(("arbitrary",)),
        name="cumsum_rows",
    )(x)


def _qk(q, k):
    return lax.dot_general(q, k, (((1,), (1,)), ((), ())), preferred_element_type=F32)


def _ones_column(rows):
    lane = lax.broadcasted_iota(jnp.int32, (rows, LANES), 1)
    return jnp.where(lane == 0, 1.0, 0.0).astype(BF16)


def _online_update(s, v1, m, acc):
    m_new = jnp.maximum(m, jnp.max(s, axis=-1, keepdims=True))
    p = jnp.exp2(s - m_new)
    alpha = jnp.exp2(m - m_new)
    acc = alpha * acc + jnp.dot(p.astype(BF16), v1, preferred_element_type=F32)
    return m_new, acc


def _softmax_state(tq):
    return (jnp.full((tq, 1), NEG_INF, F32), jnp.zeros((tq, 2 * HEAD_DIM), F32))


def _normalize(acc):
    return acc[:, :HEAD_DIM] / acc[:, HEAD_DIM:HEAD_DIM + 1]


ATTN_TQ = 1024
ATTN_SPAN = 2


def _attn_a_kernel(q_ref, k_ref, v_ref, fa_ref, g_ref, o_ref, *, tq, tk):
    qi = pl.program_id(1)
    lane = lax.broadcasted_iota(jnp.int32, (tq, LANES), 1)
    ones = jnp.where(lane < DECAY_PIECES, 1.0, 0.0).astype(BF16)
    q = jnp.concatenate([q_ref[...], ones], axis=1)

    def update(s, v, m, l, acc):
        m_new = jnp.maximum(m, jnp.max(s, axis=-1, keepdims=True))
        p = jnp.exp2(s - m_new)
        alpha = jnp.exp2(m - m_new)
        l = alpha * l + functools.reduce(jnp.add, [p[:, i:i + LANES] for i in range(0, s.shape[1], LANES)])
        return m_new, l, alpha * acc + jnp.dot(p.astype(BF16), v, preferred_element_type=F32)

    def span(first_block, width, carry, masked):
        start = pl.multiple_of(first_block * tq, tq)
        k = jnp.concatenate([k_ref[pl.ds(start, width), :], fa_ref[pl.ds(start, width), :]], axis=1)
        s = _qk(q, k)
        if masked:
            r = lax.broadcasted_iota(jnp.int32, (tq, 1), 0)
            c = lax.broadcasted_iota(jnp.int32, (1, width), 1)
            s = jnp.where(c <= r, s, NEG_INF)
        return update(s, v_ref[pl.ds(start, width), :], *carry)

    per = tk // tq
    init = (jnp.full((tq, 1), NEG_INF, F32), jnp.zeros((tq, LANES), F32), jnp.zeros((tq, HEAD_DIM), F32))
    carry = lax.fori_loop(0, qi // per, lambda i, cr: span(i * per, tk, cr, False), init)
    if per > 1:
        assert per == 2
        carry = lax.fori_loop(0, qi % per, lambda i, cr: span(qi - 1, tq, cr, False), carry)
    _, l, acc = span(qi, tq, carry, True)
    o_ref[...] = _rms(acc / jnp.sum(l, axis=-1, keepdims=True), g_ref[...]).astype(BF16)


def _attn_a(proj, decay, g):
    s = proj.shape[0]
    tq = _tile(s, ATTN_TQ)
    nh = N_HEADS_A
    kern = functools.partial(_attn_a_kernel, tq=tq, tk=tq * ATTN_SPAN)
    return pl.pallas_call(
        kern,
        grid=(nh, s // tq),
        in_specs=[
            pl.BlockSpec((tq, HEAD_DIM), lambda h, i: (i, h)),
            pl.BlockSpec((s, HEAD_DIM), lambda h, i: (0, nh + h)),
            pl.BlockSpec((s, HEAD_DIM), lambda h, i: (0, 2 * nh + h)),
            pl.BlockSpec((None, s, LANES), lambda h, i: (h, 0, 0)),
            pl.BlockSpec((1, HEAD_DIM), lambda h, i: (0, 0)),
        ],
        out_specs=pl.BlockSpec((tq, HEAD_DIM), lambda h, i: (i, h)),
        out_shape=jax.ShapeDtypeStruct((s, WIDTH_A), BF16),
        compiler_params=_cparams(("arbitrary", "arbitrary")),
        name="attn_forget",
    )(proj, proj, proj, decay, g)


TQB = 256
BAND = LEFT_CHUNKS * CHUNK
NWIN = (BAND + TQB) // TQB
BAND_STEP_ROWS = 1024


def _attn_b_kernel(q_ref, k_ref, v_ref, bias_ref, g_ref, o_ref, *, tiles):
    ones_col = _ones_column(TQB)
    for t in range(tiles):
        qt = pl.program_id(1) * tiles + t
        q = q_ref[t * TQB:(t + 1) * TQB, :]
        starts, ss = [], []
        for w in range(NWIN):
            kb = qt - (NWIN - 1) + w
            starts.append(pl.multiple_of(jnp.maximum(kb, 0) * TQB, TQB))
            sw = _qk(q, k_ref[pl.ds(starts[w], TQB), :]) + bias_ref[:, w * TQB:(w + 1) * TQB]
            ss.append(jnp.where(kb >= 0, sw, NEG_INF))
        m = functools.reduce(jnp.maximum, [jnp.max(sw, axis=-1, keepdims=True) for sw in ss])
        acc = jnp.zeros((TQB, 2 * HEAD_DIM), F32)
        for w in range(NWIN):
            v1 = jnp.concatenate([v_ref[pl.ds(starts[w], TQB), :], ones_col], axis=1)
            acc = acc + jnp.dot(jnp.exp2(ss[w] - m).astype(BF16), v1, preferred_element_type=F32)
        o_ref[t * TQB:(t + 1) * TQB, :] = _rms(_normalize(acc), g_ref[...]).astype(BF16)


def _band_bias_table(rel_bias):
    w = NWIN * TQB
    n = TQB + w
    u = np.arange(n) - (TQB - 1)
    dist = (NWIN - 1) * TQB - u
    idx = np.clip(dist, -REL_CLIP, REL_CLIP) + REL_CLIP
    diag = rel_bias.astype(F32)[:, idx] * LOG2E
    diag = jnp.roll(diag, -(TQB - 1), axis=1)
    nh = rel_bias.shape[0]
    skew = jnp.tile(diag, (1, TQB))[:, :TQB * (n - 1)].reshape(nh, TQB, n - 1)[:, :, :w]
    qpos = np.arange(TQB)[:, None]
    kpos = np.arange(w)[None, :] - (NWIN - 1) * TQB
    qc = qpos // CHUNK
    kc = np.floor_divide(kpos, CHUNK)
    valid = (kc <= qc) & (kc >= qc - LEFT_CHUNKS)
    return jnp.where(jnp.asarray(valid)[None], skew, NEG_INF)


def _attn_b(proj, bias_tbl, g):
    s = proj.shape[0]
    nh = N_HEADS_B
    c0 = 3 * N_HEADS_A
    tq = _tile(s, BAND_STEP_ROWS)
    return pl.pallas_call(
        functools.partial(_attn_b_kernel, tiles=tq // TQB),
        grid=(nh, s // tq),
        in_specs=[
            pl.BlockSpec((tq, HEAD_DIM), lambda h, i: (i, c0 + h)),
            pl.BlockSpec((s, HEAD_DIM), lambda h, i: (0, c0 + nh + h)),
            pl.BlockSpec((s, HEAD_DIM), lambda h, i: (0, c0 + 2 * nh + h)),
            pl.BlockSpec((None, TQB, NWIN * TQB), lambda h, i: (h, 0, 0)),
            pl.BlockSpec((1, HEAD_DIM), lambda h, i: (0, 0)),
        ],
        out_specs=pl.BlockSpec((tq, HEAD_DIM), lambda h, i: (i, h)),
        out_shape=jax.ShapeDtypeStruct((s, WIDTH_B), BF16),
        compiler_params=_cparams(("arbitrary", "arbitrary")),
        name="attn_band",
    )(proj, proj, proj, bias_tbl, g)


def _attn_c_kernel(q_ref, k_ref, v_ref, lam_ref, g_ref, o_ref, *, tq, tk, lam_init):
    qi = pl.program_id(1)
    q = q_ref[...]
    lane = lax.broadcasted_iota(jnp.int32, (1, HEAD_DIM), 1)
    zero = jnp.zeros_like(q)
    q0 = jnp.where(lane < DIFF_DIM, q, zero)
    q1 = jnp.where(lane >= DIFF_DIM, q, zero)
    ones_cols = {w: _ones_column(w) for w in {tq, tk}}

    def span(first_block, width, carry, masked):
        start = pl.multiple_of(first_block * tq, tq)
        k = k_ref[pl.ds(start, width), :]
        v = jnp.concatenate([v_ref[pl.ds(start, width), :], ones_cols[width]], axis=1)
        s0 = _qk(q0, k)
        s1 = _qk(q1, k)
        if masked:
            r = lax.broadcasted_iota(jnp.int32, (tq, 1), 0) // CHUNK
            c = lax.broadcasted_iota(jnp.int32, (1, width), 1) // CHUNK
            ok = c <= r
            s0 = jnp.where(ok, s0, NEG_INF)
            s1 = jnp.where(ok, s1, NEG_INF)
        return _online_update(s0, v, *carry[:2]) + _online_update(s1, v, *carry[2:])

    per = tk // tq
    st = _softmax_state(tq)
    carry = lax.fori_loop(0, qi // per, lambda i, cr: span(i * per, tk, cr, False), st + st)
    if per > 1:
        assert per == 2
        carry = lax.fori_loop(0, qi % per, lambda i, cr: span(qi - 1, tq, cr, False), carry)
    _, a0, _, a1 = span(qi, tq, carry, True)
    lam = lam_ref[...]
    lam_full = (jnp.exp(jnp.sum(lam[0:1] * lam[1:2], axis=-1, keepdims=True))
                - jnp.exp(jnp.sum(lam[2:3] * lam[3:4], axis=-1, keepdims=True)) + lam_init)
    o = _normalize(a0) - lam_full * _normalize(a1)
    o_ref[...] = (_rms(o, g_ref[...]) * (1.0 - lam_init)).astype(BF16)


def _attn_c(proj, lam, g, lam_init):
    s = proj.shape[0]
    tq = _tile(s, ATTN_TQ)
    nh = N_HEADS_C
    c0 = 3 * N_HEADS_A + 3 * N_HEADS_B
    kern = functools.partial(_attn_c_kernel, tq=tq, tk=tq * ATTN_SPAN, lam_init=lam_init)
    return pl.pallas_call(
        kern,
        grid=(nh, s // tq),
        in_specs=[
            pl.BlockSpec((tq, HEAD_DIM), lambda h, i: (i, c0 + h)),
            pl.BlockSpec((s, HEAD_DIM), lambda h, i: (0, c0 + nh + h)),
            pl.BlockSpec((s, HEAD_DIM), lambda h, i: (0, c0 + 2 * nh + h)),
            pl.BlockSpec((4, DIFF_DIM), lambda h, i: (0, 0)),
            pl.BlockSpec((1, HEAD_DIM), lambda h, i: (0, 0)),
        ],
        out_specs=pl.BlockSpec((tq, HEAD_DIM), lambda h, i: (i, h)),
        out_shape=jax.ShapeDtypeStruct((s, WIDTH_C), BF16),
        compiler_params=_cparams(("arbitrary", "arbitrary")),
        name="attn_diff",
    )(proj, proj, proj, lam, g)


def _residual(x, y, g_post, gate):
    return x + gate * _rms(y, g_post)


def _outproj_kernel(oa_ref, ob_ref, oc_ref, wa_ref, wb_ref, wc_ref, x_ref, gp_ref, gate_ref,
                    gn_ref, sc_ref, sh_ref, xo_ref, ho_ref):
    y = (jnp.dot(oa_ref[...], wa_ref[...], preferred_element_type=F32)
         + jnp.dot(ob_ref[...], wb_ref[...], preferred_element_type=F32)
         + jnp.dot(oc_ref[...], wc_ref[...], preferred_element_type=F32))
    xn = _residual(x_ref[...], y, gp_ref[...], gate_ref[...])
    xo_ref[...] = xn
    ho_ref[...] = (_rms(xn, gn_ref[...]) * (1.0 + sc_ref[...]) + sh_ref[...]).astype(ho_ref.dtype)


def _outproj(oa, ob, oc, w_o, x, g_post, gate, g_next, sc, sh, h_dtype):
    s, d = x.shape
    tm = _tile(s, 256)
    row = lambda i: (i, 0)
    const = lambda i: (0, 0)
    assert WIDTH_A == WIDTH_B and (WIDTH_A + WIDTH_B) % WIDTH_C == 0
    return pl.pallas_call(
        _outproj_kernel,
        grid=(s // tm,),
        in_specs=[
            pl.BlockSpec((tm, WIDTH_A), row),
            pl.BlockSpec((tm, WIDTH_B), row),
            pl.BlockSpec((tm, WIDTH_C), row),
            pl.BlockSpec((WIDTH_A, d), lambda i: (0, 0)),
            pl.BlockSpec((WIDTH_B, d), lambda i: (1, 0)),
            pl.BlockSpec((WIDTH_C, d), lambda i: ((WIDTH_A + WIDTH_B) // WIDTH_C, 0)),
            pl.BlockSpec((tm, d), row),
            pl.BlockSpec((1, d), const),
            pl.BlockSpec((1, d), const),
            pl.BlockSpec((1, d), const),
            pl.BlockSpec((1, d), const),
            pl.BlockSpec((1, d), const),
        ],
        out_specs=[pl.BlockSpec((tm, d), row), pl.BlockSpec((tm, d), row)],
        out_shape=[jax.ShapeDtypeStruct((s, d), F32), jax.ShapeDtypeStruct((s, d), h_dtype)],
        compiler_params=_cparams(("arbitrary",)),
        name="outproj_residual",
    )(oa, ob, oc, w_o, w_o, w_o, x, g_post, gate, g_next, sc, sh)


def _gateup_kernel(h_ref, wg_ref, wu_ref, o_ref):
    h = h_ref[...]
    a = jnp.dot(h, wg_ref[...], preferred_element_type=F32)
    u = jnp.dot(h, wu_ref[...], preferred_element_type=F32)
    o_ref[...] = ((a * jax.nn.sigmoid(a)) * u).astype(BF16)


def _gateup(h, wg, wu):
    s, d = h.shape
    f = wg.shape[1]
    tm = _tile(s, 1024)
    tf = _tile(f, 512)
    return pl.pallas_call(
        _gateup_kernel,
        grid=(s // tm, f // tf),
        in_specs=[
            pl.BlockSpec((tm, d), lambda i, j: (i, 0)),
            pl.BlockSpec((d, tf), lambda i, j: (0, j)),
            pl.BlockSpec((d, tf), lambda i, j: (0, j)),
        ],
        out_specs=pl.BlockSpec((tm, tf), lambda i, j: (i, j)),
        out_shape=jax.ShapeDtypeStruct((s, f), BF16),
        compiler_params=_cparams(("arbitrary", "arbitrary")),
        name="ffn_gateup",
    )(h, wg, wu)


def _down_kernel(a_ref, w_ref, x_ref, gp_ref, gate_ref, o_ref):
    y = jnp.dot(a_ref[...], w_ref[...], preferred_element_type=F32)
    o_ref[...] = _residual(x_ref[...], y, gp_ref[...], gate_ref[...])


def _down(act, wd, x, g_post, gate):
    s, f = act.shape
    d = wd.shape[1]
    tm = _tile(s, 256)
    return pl.pallas_call(
        _down_kernel,
        grid=(s // tm,),
        in_specs=[
            pl.BlockSpec((tm, f), lambda i: (i, 0)),
            pl.BlockSpec((f, d), lambda i: (0, 0), pipeline_mode=pl.Buffered(1)),
            pl.BlockSpec((tm, d), lambda i: (i, 0)),
            pl.BlockSpec((1, d), lambda i: (0, 0)),
            pl.BlockSpec((1, d), lambda i: (0, 0)),
        ],
        out_specs=pl.BlockSpec((tm, d), lambda i: (i, 0)),
        out_shape=jax.ShapeDtypeStruct((s, d), F32),
        compiler_params=_cparams(("arbitrary",)),
        name="ffn_down_residual",
    )(act, wd, x, g_post, gate)


def _split2(a):
    hi = a.astype(BF16)
    return hi, (a - hi.astype(F32)).astype(BF16)


def _router_kernel(h_ref, w_ref, b_ref, route_ref, mh_ref):
    hh, hl = _split2(h_ref[...])
    wh, wl = _split2(w_ref[...])
    logits = (jnp.dot(hh, wh, preferred_element_type=F32)
              + jnp.dot(hh, wl, preferred_element_type=F32)
              + jnp.dot(hl, wh, preferred_element_type=F32)) + b_ref[...]
    lane = lax.broadcasted_iota(jnp.int32, logits.shape, 1)
    logits = jnp.where(lane < N_EXPERTS, logits, NEG_INF)
    v1 = jnp.max(logits, axis=-1, keepdims=True)
    i1 = jnp.min(jnp.where(logits == v1, lane, LANES), axis=-1, keepdims=True)
    rest = jnp.where(lane == i1, NEG_INF, logits)
    v2 = jnp.max(rest, axis=-1, keepdims=True)
    i2 = jnp.min(jnp.where(rest == v2, lane, LANES), axis=-1, keepdims=True)
    e = jnp.exp(v2 - v1)
    g1 = 1.0 / (1.0 + e)
    g2 = e / (1.0 + e)
    route_ref[...] = jnp.where(lane == 0, i1.astype(F32),
                               jnp.where(lane == 1, i2.astype(F32),
                                         jnp.where(lane == 2, g1, jnp.where(lane == 3, g2, 0.0))))
    mh_ref[...] = jnp.where(jnp.logical_or(lane == i1, lane == i2), 1.0, 0.0)


def _router(h, rw_pad, rb_pad):
    s, d = h.shape
    tm = _tile(s, 512)
    return pl.pallas_call(
        _router_kernel,
        grid=(s // tm,),
        in_specs=[
            pl.BlockSpec((tm, d), lambda i: (i, 0)),
            pl.BlockSpec((d, LANES), lambda i: (0, 0)),
            pl.BlockSpec((1, LANES), lambda i: (0, 0)),
        ],
        out_specs=[pl.BlockSpec((tm, LANES), lambda i: (i, 0)),
                   pl.BlockSpec((tm, LANES), lambda i: (i, 0))],
        out_shape=[jax.ShapeDtypeStruct((s, LANES), F32), jax.ShapeDtypeStruct((s, LANES), F32)],
        compiler_params=_cparams(("arbitrary",)),
        name="moe_router",
    )(h, rw_pad, rb_pad)


def _dispatch_kernel(pos_ref, h_ref, init_ref, o_ref, sem):
    del init_ref
    n = h_ref.shape[0]

    def copy(r, k):
        return pltpu.make_async_copy(h_ref.at[pl.ds(r, 1)], o_ref.at[pl.ds(pos_ref[0, 0, 2 * r + k], 1)], sem)

    def issue(r, c):
        copy(r, 0).start()
        copy(r, 1).start()
        return c

    def drain(r, c):
        copy(r, 0).wait()
        copy(r, 1).wait()
        return c

    lax.fori_loop(0, n, issue, 0)
    lax.fori_loop(0, n, drain, 0)


def _dispatch(h, pos, n_slots):
    s, d = h.shape
    tm = _tile(s, 256)
    init = jnp.zeros((n_slots, d), F32)
    return pl.pallas_call(
        _dispatch_kernel,
        grid=(s // tm,),
        in_specs=[
            pl.BlockSpec((1, 1, TOP_K * tm), lambda i: (i, 0, 0), memory_space=pltpu.SMEM),
            pl.BlockSpec((tm, d), lambda i: (i, 0)),
            pl.BlockSpec(memory_space=pl.ANY),
        ],
        out_specs=pl.BlockSpec(memory_space=pl.ANY),
        out_shape=jax.ShapeDtypeStruct((n_slots, d), F32),
        scratch_shapes=[pltpu.SemaphoreType.DMA(())],
        input_output_aliases={2: 0},
        compiler_params=_cparams(("arbitrary",)),
        name="moe_dispatch",
    )(pos.reshape(s // tm, 1, TOP_K * tm), h, init)


def _ggateup_kernel(te_ref, nu_ref, h_ref, wg_ref, wu_ref, o_ref, wg_scr, wu_scr):
    i = pl.program_id(1)
    used = i < nu_ref[0]
    new_weights = jnp.logical_or(i == 0, te_ref[i] != te_ref[jnp.maximum(i - 1, 0)])

    @pl.when(jnp.logical_and(used, new_weights))
    def _():
        wg_scr[...] = wg_ref[...].astype(BF16)
        wu_scr[...] = wu_ref[...].astype(BF16)

    @pl.when(used)
    def _():
        h = h_ref[...]
        a = jnp.dot(h, wg_scr[...], preferred_element_type=F32)
        u = jnp.dot(h, wu_scr[...], preferred_element_type=F32)
        o_ref[...] = ((a * jax.nn.sigmoid(a)) * u).astype(BF16)

    @pl.when(jnp.logical_not(used))
    def _():
        o_ref[...] = jnp.zeros_like(o_ref)


def _ggateup(hs, wg, wu, tile_expert, n_used, tm):
    p, d = hs.shape
    f = wg.shape[2]
    tf = _tile(f, 1024)
    return pl.pallas_call(
        _ggateup_kernel,
        grid_spec=pltpu.PrefetchScalarGridSpec(
            num_scalar_prefetch=2,
            grid=(f // tf, p // tm),
            in_specs=[
                pl.BlockSpec((tm, d), lambda j, i, te, nu: (i, 0)),
                pl.BlockSpec((None, d, tf), lambda j, i, te, nu: (te[i], 0, j)),
                pl.BlockSpec((None, d, tf), lambda j, i, te, nu: (te[i], 0, j)),
            ],
            out_specs=pl.BlockSpec((tm, tf), lambda j, i, te, nu: (i, j)),
            scratch_shapes=[pltpu.VMEM((d, tf), BF16), pltpu.VMEM((d, tf), BF16)],
        ),
        out_shape=jax.ShapeDtypeStruct((p, f), BF16),
        compiler_params=_cparams(("arbitrary", "arbitrary")),
        name="moe_gateup",
    )(tile_expert, n_used, hs, wg, wu)


def _gdown_kernel(te_ref, nu_ref, a_ref, w_ref, o_ref):
    used = pl.program_id(1) < nu_ref[0]

    @pl.when(used)
    def _():
        o_ref[...] = jnp.dot(a_ref[...], w_ref[...], preferred_element_type=F32)

    @pl.when(jnp.logical_not(used))
    def _():
        o_ref[...] = jnp.zeros_like(o_ref)


def _gdown(act, wd, tile_expert, n_used, tm):
    p, f = act.shape
    d = wd.shape[2]
    tn = _tile(d, 1024)
    return pl.pallas_call(
        _gdown_kernel,
        grid_spec=pltpu.PrefetchScalarGridSpec(
            num_scalar_prefetch=2,
            grid=(d // tn, p // tm),
            in_specs=[
                pl.BlockSpec((tm, f), lambda n, i, te, nu: (i, 0)),
                pl.BlockSpec((None, f, tn), lambda n, i, te, nu: (te[i], 0, n)),
            ],
            out_specs=pl.BlockSpec((tm, tn), lambda n, i, te, nu: (i, n)),
        ),
        out_shape=jax.ShapeDtypeStruct((p, d), F32),
        compiler_params=_cparams(("arbitrary", "arbitrary")),
        name="moe_down",
    )(tile_expert, n_used, act, wd)


def _combine_kernel(pos_ref, y_ref, route_ref, x_ref, gp_ref, gate_ref, o_ref, buf0, buf1, sem):
    n = x_ref.shape[0]

    def copy(r, k, buf):
        return pltpu.make_async_copy(y_ref.at[pl.ds(pos_ref[0, 0, 2 * r + k], 1)], buf.at[pl.ds(r, 1)], sem)

    def issue(r, c):
        copy(r, 0, buf0).start()
        copy(r, 1, buf1).start()
        return c

    def drain(r, c):
        copy(r, 0, buf0).wait()
        copy(r, 1, buf1).wait()
        return c

    lax.fori_loop(0, n, issue, 0)
    lax.fori_loop(0, n, drain, 0)
    route = route_ref[...]
    y = route[:, 2:3] * buf0[...] + route[:, 3:4] * buf1[...]
    o_ref[...] = _residual(x_ref[...], y, gp_ref[...], gate_ref[...])


def _combine(ys, pos, route, x, g_post, gate):
    s, d = x.shape
    tm = _tile(s, 256)
    return pl.pallas_call(
        _combine_kernel,
        grid=(s // tm,),
        in_specs=[
            pl.BlockSpec((1, 1, TOP_K * tm), lambda i: (i, 0, 0), memory_space=pltpu.SMEM),
            pl.BlockSpec(memory_space=pl.ANY),
            pl.BlockSpec((tm, LANES), lambda i: (i, 0)),
            pl.BlockSpec((tm, d), lambda i: (i, 0)),
            pl.BlockSpec((1, d), lambda i: (0, 0)),
            pl.BlockSpec((1, d), lambda i: (0, 0)),
        ],
        out_specs=pl.BlockSpec((tm, d), lambda i: (i, 0)),
        out_shape=jax.ShapeDtypeStruct((s, d), F32),
        scratch_shapes=[pltpu.VMEM((tm, d), F32), pltpu.VMEM((tm, d), F32), pltpu.SemaphoreType.DMA(())],
        compiler_params=_cparams(("arbitrary",)),
        name="moe_combine_residual",
    )(pos.reshape(s // tm, 1, TOP_K * tm), ys, route, x, g_post, gate)


MOE_TM = 512


def _moe(h, x, router_w, router_b, wg, wu, wd, g_post, gate):
    s, d = h.shape
    pad = lambda a: jnp.pad(a.astype(F32), ((0, 0), (0, LANES - N_EXPERTS)))
    route, mh = _router(h, pad(router_w), pad(router_b.reshape(1, N_EXPERTS)))
    csum, = _cumsum_rows(mh)
    tm = _tile(TOP_K * s, MOE_TM)
    counts = csum[s - 1, :N_EXPERTS].astype(jnp.int32)
    padded = ((counts + tm - 1) // tm) * tm
    ends = jnp.cumsum(padded)
    starts = ends - padded
    ids = route[:, :TOP_K].astype(jnp.int32)
    slot_of = starts[None, :] + (csum[:, :N_EXPERTS] - mh[:, :N_EXPERTS]).astype(jnp.int32)
    pos = jnp.take_along_axis(slot_of, ids, axis=1)
    n_tiles = (TOP_K * s) // tm + N_EXPERTS
    n_used = (ends[N_EXPERTS - 1] // tm).astype(jnp.int32).reshape(1)
    tile_start = jnp.minimum(jnp.arange(n_tiles, dtype=jnp.int32), n_used[0] - 1) * tm
    tile_expert = jnp.sum(tile_start[:, None] >= ends[None, :], axis=1).astype(jnp.int32)
    hs = _dispatch(h, pos, n_tiles * tm).astype(BF16)
    act = _ggateup(hs, wg, wu, tile_expert, n_used, tm)
    ys = _gdown(act, wd, tile_expert, n_used, tm)
    return _combine(ys, pos, route, x, g_post, gate)


def kernel(x, c, positions, mod_w, mod_b, norm_g, w_in, b_f, rel_bias, lam, onorm, w_o,
           ffn_wg, ffn_wu, ffn_wd, router_w, router_b, exp_wg, exp_wu, exp_wd):
    b, s, d = x.shape
    assert b == 1 and s % TQB == 0
    depth = mod_w.shape[0]
    xs = x.reshape(s, d)
    pos_col = positions.reshape(s, 1).astype(F32)
    mods = _mods(c, mod_w, mod_b)

    f0 = 3 * WIDTH_A
    scale_ab = HEAD_DIM ** -0.5 * LOG2E
    scale_c = DIFF_DIM ** -0.5 * LOG2E
    colscale = np.ones((1, w_in.shape[2] - N_HEADS_A), np.float32)
    colscale[0, 0:WIDTH_A] = scale_ab
    colscale[0, f0:f0 + WIDTH_B] = scale_ab
    colscale[0, ROPE_COL_LO:ROPE_COL_LO + WIDTH_C] = scale_c
    colscale = jnp.asarray(colscale)

    for l in range(depth):
        sh_m, sc_m, g_m, sh_f, sc_f, g_f = [mods[l, :, i * d:(i + 1) * d] for i in range(6)]
        ng = lambda i: norm_g[l, i].reshape(1, d).astype(F32)
        w_main = jnp.concatenate([w_in[l, :, :f0], w_in[l, :, f0 + N_HEADS_A:]], axis=1).astype(BF16)
        w_f = jnp.pad(w_in[l, :, f0:f0 + N_HEADS_A], ((0, 0), (0, LANES - N_HEADS_A))).astype(BF16)
        bf = jnp.pad(b_f[l].astype(F32).reshape(1, N_HEADS_A), ((0, 0), (0, LANES - N_HEADS_A)))
        proj, logf = _inproj(xs, ng(0), sc_m, sh_m, w_main, colscale, w_f, bf, pos_col)
        _, decay = _cumsum_rows(logf, decay_heads=N_HEADS_A)
        on = lambda i: onorm[l, i].reshape(1, HEAD_DIM).astype(F32)
        lam_init = 0.8 - 0.6 * math.exp(-0.3 * l)
        oa = _attn_a(proj, decay, on(0))
        ob = _attn_b(proj, _band_bias_table(rel_bias[l]), on(1))
        oc = _attn_c(proj, lam[l].astype(F32), on(2), lam_init)
        moe = l % 2 == 1
        xs, h2 = _outproj(oa, ob, oc, w_o[l].astype(BF16), xs, ng(1), g_m, ng(2), sc_f, sh_f,
                          F32 if moe else BF16)
        i = l // 2
        if moe:
            xs = _moe(h2, xs, router_w[i], router_b[i], exp_wg[i], exp_wu[i], exp_wd[i].astype(BF16), ng(3), g_f)
        else:
            act = _gateup(h2, ffn_wg[i].astype(BF16), ffn_wu[i].astype(BF16))
            xs = _down(act, ffn_wd[i].astype(BF16), xs, ng(3), g_f)
    return xs.reshape(b, s, d)
```

```python
import functools
import math

import numpy as np
import jax
import jax.numpy as jnp
from jax import lax
from jax.experimental import pallas as pl
from jax.experimental.pallas import tpu as pltpu

F32 = jnp.float32
BF16 = jnp.bfloat16

CHUNK = 64
HEAD_DIM = 128
N_HEADS_A = 6
N_HEADS_B = 6
N_HEADS_C = 4
WIDTH_A = N_HEADS_A * HEAD_DIM
WIDTH_B = N_HEADS_B * HEAD_DIM
WIDTH_C = N_HEADS_C * HEAD_DIM
DIFF_DIM = HEAD_DIM // 2
LEFT_CHUNKS = 8
REL_CLIP = 256
ROPE_THETA = 500000.0
ROPE_DIM = DIFF_DIM // 4
N_EXPERTS = 8
TOP_K = 2
EPS = 1e-6
NEG_INF = -1e30
LOG2E = math.log2(math.e)

LANES = 128
VMEM_LIMIT = 56 * 1024 * 1024


def _cparams(sem):
    return pltpu.CompilerParams(dimension_semantics=sem, vmem_limit_bytes=VMEM_LIMIT)


def _tile(n, pref):
    t = min(n, pref)
    while n % t:
        t //= 2
    return t


def _rms(x, g):
    return (x * lax.rsqrt(jnp.mean(x * x, axis=-1, keepdims=True) + EPS)) * g


def _mods_kernel(c_ref, w_ref, b_ref, o_ref):
    c = c_ref[...]
    cond = c * jax.nn.sigmoid(c)
    o_ref[...] = jnp.sum(w_ref[...] * cond, axis=0, keepdims=True) + b_ref[...]


def _mods(c, mod_w, mod_b):
    depth, d, n = mod_w.shape
    tn = _tile(n, 1024)
    return pl.pallas_call(
        _mods_kernel,
        grid=(depth, n // tn),
        in_specs=[
            pl.BlockSpec((d, 1), lambda l, j: (0, 0)),
            pl.BlockSpec((None, d, tn), lambda l, j: (l, 0, j)),
            pl.BlockSpec((None, 1, tn), lambda l, j: (l, 0, j)),
        ],
        out_specs=pl.BlockSpec((None, 1, tn), lambda l, j: (l, 0, j)),
        out_shape=jax.ShapeDtypeStruct((depth, 1, n), F32),
        compiler_params=_cparams(("arbitrary", "arbitrary")),
        name="adaln_mods",
    )(c.reshape(d, 1), mod_w, mod_b.reshape(depth, 1, n))


ROPE_COL_LO = 3 * WIDTH_A + 3 * WIDTH_B
ROPE_COL_HI = ROPE_COL_LO + 2 * WIDTH_C
INPROJ_TN = 512


def _inproj_kernel(x_ref, g_ref, sc_ref, sh_ref, w_ref, cs_ref, wf_ref, bf_ref, pos_ref,
                   inv_ref, m1_ref, m2_ref, o_ref, lf_ref, h_scr, cos_scr, sin_scr):
    j = pl.program_id(1)

    @pl.when(j == 0)
    def _():
        h = _rms(x_ref[...], g_ref[...]) * (1.0 + sc_ref[...]) + sh_ref[...]
        hb = h.astype(BF16)
        h_scr[...] = hb
        fa = jnp.dot(hb, wf_ref[...], preferred_element_type=F32) + bf_ref[...]
        lf_ref[...] = jax.nn.log_sigmoid(fa)
        ang = pos_ref[...] * inv_ref[...]
        cos_scr[...] = jnp.cos(ang)
        sin_scr[...] = jnp.sin(ang)

    y = jnp.dot(h_scr[...], w_ref[...], preferred_element_type=F32) * cs_ref[...]
    is_rope = jnp.logical_and(j >= ROPE_COL_LO // INPROJ_TN, j < ROPE_COL_HI // INPROJ_TN)

    @pl.when(is_rope)
    def _():
        c = cos_scr[...]
        s = sin_scr[...]
        m1 = m1_ref[...]
        m2 = m2_ref[...]
        for hh in range(INPROJ_TN // LANES):
            yh = y[:, hh * LANES:(hh + 1) * LANES]
            up = pltpu.roll(yh, LANES - ROPE_DIM // 2, 1)
            dn = pltpu.roll(yh, ROPE_DIM // 2, 1)
            o_ref[:, hh * LANES:(hh + 1) * LANES] = (yh * c + s * (m1 * up + m2 * dn)).astype(BF16)

    @pl.when(jnp.logical_not(is_rope))
    def _():
        o_ref[...] = y.astype(BF16)


def _rope_lane_tables():
    lane = np.arange(LANES)
    r = lane % DIFF_DIM
    half = ROPE_DIM // 2
    inv = np.where(r < ROPE_DIM, ROPE_THETA ** (-((r % half) * 2.0 / ROPE_DIM)), 0.0)
    m1 = np.where(r < half, -1.0, 0.0)
    m2 = np.where((r >= half) & (r < ROPE_DIM), 1.0, 0.0)
    f = lambda a: jnp.asarray(a.reshape(1, LANES), F32)
    return f(inv), f(m1), f(m2)


def _inproj(x, g, sc, sh, w_main, colscale, w_f, b_f, pos_col):
    s, d = x.shape
    n = w_main.shape[1]
    tm = _tile(s, 1024)
    tn = INPROJ_TN
    inv, m1, m2 = _rope_lane_tables()
    row = lambda i, j: (i, 0)
    const = lambda i, j: (0, 0)
    return pl.pallas_call(
        _inproj_kernel,
        grid=(s // tm, n // tn),
        in_specs=[
            pl.BlockSpec((tm, d), row),
            pl.BlockSpec((1, d), const),
            pl.BlockSpec((1, d), const),
            pl.BlockSpec((1, d), const),
            pl.BlockSpec((d, tn), lambda i, j: (0, j)),
            pl.BlockSpec((1, tn), lambda i, j: (0, j)),
            pl.BlockSpec((d, LANES), const),
            pl.BlockSpec((1, LANES), const),
            pl.BlockSpec((tm, 1), row),
            pl.BlockSpec((1, LANES), const),
            pl.BlockSpec((1, LANES), const),
            pl.BlockSpec((1, LANES), const),
        ],
        out_specs=[
            pl.BlockSpec((tm, tn), lambda i, j: (i, j)),
            pl.BlockSpec((tm, LANES), row),
        ],
        out_shape=[
            jax.ShapeDtypeStruct((s, n), BF16),
            jax.ShapeDtypeStruct((s, LANES), F32),
        ],
        scratch_shapes=[
            pltpu.VMEM((tm, d), BF16),
            pltpu.VMEM((tm, LANES), F32),
            pltpu.VMEM((tm, LANES), F32),
        ],
        compiler_params=_cparams(("arbitrary", "arbitrary")),
        name="inproj",
    )(x, g, sc, sh, w_main, colscale, w_f, b_f, pos_col, inv, m1, m2)


DECAY_PIECES = 3


def _split3(x):
    x1 = x.astype(BF16)
    r1 = x - x1.astype(F32)
    x2 = r1.astype(BF16)
    return x1, x2, (r1 - x2.astype(F32)).astype(BF16)


def _cumsum_kernel(x_ref, o_ref, *rest, decay_heads):
    carry = rest[-1]

    @pl.when(pl.program_id(0) == 0)
    def _():
        carry[...] = jnp.zeros_like(carry)

    x = x_ref[...]
    tb = x.shape[0]
    r = lax.broadcasted_iota(jnp.int32, (tb, tb), 0)
    c = lax.broadcasted_iota(jnp.int32, (tb, tb), 1)
    tri = jnp.where(c <= r, 1.0, 0.0).astype(BF16)
    y = sum(jnp.dot(tri, piece, preferred_element_type=F32) for piece in _split3(x)) + carry[...]
    o_ref[...] = y
    carry[...] = y[tb - 1:tb, :]
    if decay_heads:
        dec_ref = rest[0]
        lane = lax.broadcasted_iota(jnp.int32, (tb, LANES), 1)
        for h in range(decay_heads):
            hi, mid, lo = [p.astype(F32) for p in _split3(-LOG2E * y[:, h:h + 1])]
            cols = jnp.where(lane == 0, hi, jnp.where(lane == 1, mid, jnp.where(lane == 2, lo, 0.0)))
            dec_ref[h] = cols.astype(BF16)


def _cumsum_rows(x, decay_heads=0):
    s = x.shape[0]
    tb = _tile(s, 256)
    out_specs = [pl.BlockSpec((tb, LANES), lambda i: (i, 0))]
    out_shape = [jax.ShapeDtypeStruct((s, LANES), F32)]
    if decay_heads:
        out_specs.append(pl.BlockSpec((decay_heads, tb, LANES), lambda i: (0, i, 0)))
        out_shape.append(jax.ShapeDtypeStruct((decay_heads, s, LANES), BF16))
    return pl.pallas_call(
        functools.partial(_cumsum_kernel, decay_heads=decay_heads),
        grid=(s // tb,),
        in_specs=[pl.BlockSpec((tb, LANES), lambda i: (i, 0))],
        out_specs=out_specs,
        out_shape=out_shape,
        scratch_shapes=[pltpu.VMEM((1, LANES), F32)],
        compiler_params=_cparams(("arbitrary",)),
        name="cumsum_rows",
    )(x)


def _qk(q, k):
    return lax.dot_general(q, k, (((1,), (1,)), ((), ())), preferred_element_type=F32)


def _ones_column(rows):
    lane = lax.broadcasted_iota(jnp.int32, (rows, LANES), 1)
    return jnp.where(lane == 0, 1.0, 0.0).astype(BF16)


def _online_update(s, v1, m, acc):
    m_new = jnp.maximum(m, jnp.max(s, axis=-1, keepdims=True))
    p = jnp.exp2(s - m_new)
    alpha = jnp.exp2(m - m_new)
    acc = alpha * acc + jnp.dot(p.astype(BF16), v1, preferred_element_type=F32)
    return m_new, acc


def _softmax_state(tq):
    return (jnp.full((tq, 1), NEG_INF, F32), jnp.zeros((tq, 2 * HEAD_DIM), F32))


def _normalize(acc):
    return acc[:, :HEAD_DIM] / acc[:, HEAD_DIM:HEAD_DIM + 1]


ATTN_TQ = 1024
ATTN_SPAN = 2


def _attn_a_kernel(q_ref, k_ref, v_ref, fa_ref, g_ref, o_ref, *, tq, tk):
    qi = pl.program_id(1)
    lane = lax.broadcasted_iota(jnp.int32, (tq, LANES), 1)
    ones = jnp.where(lane < DECAY_PIECES, 1.0, 0.0).astype(BF16)
    q = jnp.concatenate([q_ref[...], ones], axis=1)

    def update(s, v, m, l, acc):
        m_new = jnp.maximum(m, jnp.max(s, axis=-1, keepdims=True))
        p = jnp.exp2(s - m_new)
        alpha = jnp.exp2(m - m_new)
        l = alpha * l + functools.reduce(jnp.add, [p[:, i:i + LANES] for i in range(0, s.shape[1], LANES)])
        return m_new, l, alpha * acc + jnp.dot(p.astype(BF16), v, preferred_element_type=F32)

    def span(first_block, width, carry, masked):
        start = pl.multiple_of(first_block * tq, tq)
        k = jnp.concatenate([k_ref[pl.ds(start, width), :], fa_ref[pl.ds(start, width), :]], axis=1)
        s = _qk(q, k)
        if masked:
            r = lax.broadcasted_iota(jnp.int32, (tq, 1), 0)
            c = lax.broadcasted_iota(jnp.int32, (1, width), 1)
            s = jnp.where(c <= r, s, NEG_INF)
        return update(s, v_ref[pl.ds(start, width), :], *carry)

    per = tk // tq
    init = (jnp.full((tq, 1), NEG_INF, F32), jnp.zeros((tq, LANES), F32), jnp.zeros((tq, HEAD_DIM), F32))
    carry = lax.fori_loop(0, qi // per, lambda i, cr: span(i * per, tk, cr, False), init)
    if per > 1:
        assert per == 2
        carry = lax.fori_loop(0, qi % per, lambda i, cr: span(qi - 1, tq, cr, False), carry)
    _, l, acc = span(qi, tq, carry, True)
    o_ref[...] = _rms(acc / jnp.sum(l, axis=-1, keepdims=True), g_ref[...]).astype(BF16)


def _attn_a(proj, decay, g):
    s = proj.shape[0]
    tq = _tile(s, ATTN_TQ)
    nh = N_HEADS_A
    kern = functools.partial(_attn_a_kernel, tq=tq, tk=tq * ATTN_SPAN)
    return pl.pallas_call(
        kern,
        grid=(nh, s // tq),
        in_specs=[
            pl.BlockSpec((tq, HEAD_DIM), lambda h, i: (i, h)),
            pl.BlockSpec((s, HEAD_DIM), lambda h, i: (0, nh + h)),
            pl.BlockSpec((s, HEAD_DIM), lambda h, i: (0, 2 * nh + h)),
            pl.BlockSpec((None, s, LANES), lambda h, i: (h, 0, 0)),
            pl.BlockSpec((1, HEAD_DIM), lambda h, i: (0, 0)),
        ],
        out_specs=pl.BlockSpec((tq, HEAD_DIM), lambda h, i: (i, h)),
        out_shape=jax.ShapeDtypeStruct((s, WIDTH_A), BF16),
        compiler_params=_cparams(("arbitrary", "arbitrary")),
        name="attn_forget",
    )(proj, proj, proj, decay, g)


TQB = 256
BAND = LEFT_CHUNKS * CHUNK
NWIN = (BAND + TQB) // TQB
BAND_STEP_ROWS = 2048


def _attn_b_kernel(q_ref, k_ref, v_ref, bias_ref, g_ref, o_ref, *, tiles):
    ones_col = _ones_column(TQB)
    for t in range(tiles):
        qt = pl.program_id(1) * tiles + t
        q = q_ref[t * TQB:(t + 1) * TQB, :]
        starts, ss = [], []
        for w in range(NWIN):
            kb = qt - (NWIN - 1) + w
            starts.append(pl.multiple_of(jnp.maximum(kb, 0) * TQB, TQB))
            sw = _qk(q, k_ref[pl.ds(starts[w], TQB), :]) + bias_ref[:, w * TQB:(w + 1) * TQB]
            ss.append(jnp.where(kb >= 0, sw, NEG_INF))
        m = functools.reduce(jnp.maximum, [jnp.max(sw, axis=-1, keepdims=True) for sw in ss])
        acc = jnp.zeros((TQB, 2 * HEAD_DIM), F32)
        for w in range(NWIN):
            v1 = jnp.concatenate([v_ref[pl.ds(starts[w], TQB), :], ones_col], axis=1)
            acc = acc + jnp.dot(jnp.exp2(ss[w] - m).astype(BF16), v1, preferred_element_type=F32)
        o_ref[t * TQB:(t + 1) * TQB, :] = _rms(_normalize(acc), g_ref[...]).astype(BF16)


def _band_bias_table(rel_bias):
    w = NWIN * TQB
    n = TQB + w
    u = np.arange(n) - (TQB - 1)
    dist = (NWIN - 1) * TQB - u
    idx = np.clip(dist, -REL_CLIP, REL_CLIP) + REL_CLIP
    diag = rel_bias.astype(F32)[:, idx] * LOG2E
    diag = jnp.roll(diag, -(TQB - 1), axis=1)
    nh = rel_bias.shape[0]
    skew = jnp.tile(diag, (1, TQB))[:, :TQB * (n - 1)].reshape(nh, TQB, n - 1)[:, :, :w]
    qpos = np.arange(TQB)[:, None]
    kpos = np.arange(w)[None, :] - (NWIN - 1) * TQB
    qc = qpos // CHUNK
    kc = np.floor_divide(kpos, CHUNK)
    valid = (kc <= qc) & (kc >= qc - LEFT_CHUNKS)
    return jnp.where(jnp.asarray(valid)[None], skew, NEG_INF)


def _attn_b(proj, bias_tbl, g):
    s = proj.shape[0]
    nh = N_HEADS_B
    c0 = 3 * N_HEADS_A
    tq = _tile(s, BAND_STEP_ROWS)
    return pl.pallas_call(
        functools.partial(_attn_b_kernel, tiles=tq // TQB),
        grid=(nh, s // tq),
        in_specs=[
            pl.BlockSpec((tq, HEAD_DIM), lambda h, i: (i, c0 + h)),
            pl.BlockSpec((s, HEAD_DIM), lambda h, i: (0, c0 + nh + h)),
            pl.BlockSpec((s, HEAD_DIM), lambda h, i: (0, c0 + 2 * nh + h)),
            pl.BlockSpec((None, TQB, NWIN * TQB), lambda h, i: (h, 0, 0)),
            pl.BlockSpec((1, HEAD_DIM), lambda h, i: (0, 0)),
        ],
        out_specs=pl.BlockSpec((tq, HEAD_DIM), lambda h, i: (i, h)),
        out_shape=jax.ShapeDtypeStruct((s, WIDTH_B), BF16),
        compiler_params=_cparams(("arbitrary", "arbitrary")),
        name="attn_band",
    )(proj, proj, proj, bias_tbl, g)


def _attn_c_kernel(q_ref, k_ref, v_ref, lam_ref, g_ref, o_ref, *, tq, tk, lam_init):
    qi = pl.program_id(1)
    q = q_ref[...]
    lane = lax.broadcasted_iota(jnp.int32, (1, HEAD_DIM), 1)
    zero = jnp.zeros_like(q)
    q0 = jnp.where(lane < DIFF_DIM, q, zero)
    q1 = jnp.where(lane >= DIFF_DIM, q, zero)
    ones_cols = {w: _ones_column(w) for w in {tq, tk}}

    def span(first_block, width, carry, masked):
        start = pl.multiple_of(first_block * tq, tq)
        k = k_ref[pl.ds(start, width), :]
        v = jnp.concatenate([v_ref[pl.ds(start, width), :], ones_cols[width]], axis=1)
        s0 = _qk(q0, k)
        s1 = _qk(q1, k)
        if masked:
            r = lax.broadcasted_iota(jnp.int32, (tq, 1), 0) // CHUNK
            c = lax.broadcasted_iota(jnp.int32, (1, width), 1) // CHUNK
            ok = c <= r
            s0 = jnp.where(ok, s0, NEG_INF)
            s1 = jnp.where(ok, s1, NEG_INF)
        return _online_update(s0, v, *carry[:2]) + _online_update(s1, v, *carry[2:])

    per = tk // tq
    st = _softmax_state(tq)
    carry = lax.fori_loop(0, qi // per, lambda i, cr: span(i * per, tk, cr, False), st + st)
    if per > 1:
        assert per == 2
        carry = lax.fori_loop(0, qi % per, lambda i, cr: span(qi - 1, tq, cr, False), carry)
    _, a0, _, a1 = span(qi, tq, carry, True)
    lam = lam_ref[...]
    lam_full = (jnp.exp(jnp.sum(lam[0:1] * lam[1:2], axis=-1, keepdims=True))
                - jnp.exp(jnp.sum(lam[2:3] * lam[3:4], axis=-1, keepdims=True)) + lam_init)
    o = _normalize(a0) - lam_full * _normalize(a1)
    o_ref[...] = (_rms(o, g_ref[...]) * (1.0 - lam_init)).astype(BF16)


def _attn_c(proj, lam, g, lam_init):
    s = proj.shape[0]
    tq = _tile(s, ATTN_TQ)
    nh = N_HEADS_C
    c0 = 3 * N_HEADS_A + 3 * N_HEADS_B
    kern = functools.partial(_attn_c_kernel, tq=tq, tk=tq * ATTN_SPAN, lam_init=lam_init)
    return pl.pallas_call(
        kern,
        grid=(nh, s // tq),
        in_specs=[
            pl.BlockSpec((tq, HEAD_DIM), lambda h, i: (i, c0 + h)),
            pl.BlockSpec((s, HEAD_DIM), lambda h, i: (0, c0 + nh + h)),
            pl.BlockSpec((s, HEAD_DIM), lambda h, i: (0, c0 + 2 * nh + h)),
            pl.BlockSpec((4, DIFF_DIM), lambda h, i: (0, 0)),
            pl.BlockSpec((1, HEAD_DIM), lambda h, i: (0, 0)),
        ],
        out_specs=pl.BlockSpec((tq, HEAD_DIM), lambda h, i: (i, h)),
        out_shape=jax.ShapeDtypeStruct((s, WIDTH_C), BF16),
        compiler_params=_cparams(("arbitrary", "arbitrary")),
        name="attn_diff",
    )(proj, proj, proj, lam, g)


def _residual(x, y, g_post, gate):
    return x + gate * _rms(y, g_post)


def _outproj_kernel(oa_ref, ob_ref, oc_ref, wa_ref, wb_ref, wc_ref, x_ref, gp_ref, gate_ref,
                    gn_ref, sc_ref, sh_ref, xo_ref, ho_ref):
    y = (jnp.dot(oa_ref[...], wa_ref[...], preferred_element_type=F32)
         + jnp.dot(ob_ref[...], wb_ref[...], preferred_element_type=F32)
         + jnp.dot(oc_ref[...], wc_ref[...], preferred_element_type=F32))
    xn = _residual(x_ref[...], y, gp_ref[...], gate_ref[...])
    xo_ref[...] = xn
    ho_ref[...] = (_rms(xn, gn_ref[...]) * (1.0 + sc_ref[...]) + sh_ref[...]).astype(ho_ref.dtype)


def _outproj(oa, ob, oc, w_o, x, g_post, gate, g_next, sc, sh, h_dtype):
    s, d = x.shape
    tm = _tile(s, 512)
    row = lambda i: (i, 0)
    const = lambda i: (0, 0)
    assert WIDTH_A == WIDTH_B and (WIDTH_A + WIDTH_B) % WIDTH_C == 0
    return pl.pallas_call(
        _outproj_kernel,
        grid=(s // tm,),
        in_specs=[
            pl.BlockSpec((tm, WIDTH_A), row),
            pl.BlockSpec((tm, WIDTH_B), row),
            pl.BlockSpec((tm, WIDTH_C), row),
            pl.BlockSpec((WIDTH_A, d), lambda i: (0, 0)),
            pl.BlockSpec((WIDTH_B, d), lambda i: (1, 0)),
            pl.BlockSpec((WIDTH_C, d), lambda i: ((WIDTH_A + WIDTH_B) // WIDTH_C, 0)),
            pl.BlockSpec((tm, d), row),
            pl.BlockSpec((1, d), const),
            pl.BlockSpec((1, d), const),
            pl.BlockSpec((1, d), const),
            pl.BlockSpec((1, d), const),
            pl.BlockSpec((1, d), const),
        ],
        out_specs=[pl.BlockSpec((tm, d), row), pl.BlockSpec((tm, d), row)],
        out_shape=[jax.ShapeDtypeStruct((s, d), F32), jax.ShapeDtypeStruct((s, d), h_dtype)],
        compiler_params=_cparams(("arbitrary",)),
        name="outproj_residual",
    )(oa, ob, oc, w_o, w_o, w_o, x, g_post, gate, g_next, sc, sh)


def _gateup_kernel(h_ref, wg_ref, wu_ref, o_ref):
    h = h_ref[...]
    a = jnp.dot(h, wg_ref[...], preferred_element_type=F32)
    u = jnp.dot(h, wu_ref[...], preferred_element_type=F32)
    o_ref[...] = ((a * jax.nn.sigmoid(a)) * u).astype(BF16)


def _gateup(h, wg, wu):
    s, d = h.shape
    f = wg.shape[1]
    tm = _tile(s, 1024)
    tf = _tile(f, 512)
    return pl.pallas_call(
        _gateup_kernel,
        grid=(s // tm, f // tf),
        in_specs=[
            pl.BlockSpec((tm, d), lambda i, j: (i, 0)),
            pl.BlockSpec((d, tf), lambda i, j: (0, j)),
            pl.BlockSpec((d, tf), lambda i, j: (0, j)),
        ],
        out_specs=pl.BlockSpec((tm, tf), lambda i, j: (i, j)),
        out_shape=jax.ShapeDtypeStruct((s, f), BF16),
        compiler_params=_cparams(("arbitrary", "arbitrary")),
        name="ffn_gateup",
    )(h, wg, wu)


def _down_kernel(a_ref, w_ref, x_ref, gp_ref, gate_ref, o_ref):
    y = jnp.dot(a_ref[...], w_ref[...], preferred_element_type=F32)
    o_ref[...] = _residual(x_ref[...], y, gp_ref[...], gate_ref[...])


def _down(act, wd, x, g_post, gate):
    s, f = act.shape
    d = wd.shape[1]
    tm = _tile(s, 256)
    return pl.pallas_call(
        _down_kernel,
        grid=(s // tm,),
        in_specs=[
            pl.BlockSpec((tm, f), lambda i: (i, 0)),
            pl.BlockSpec((f, d), lambda i: (0, 0), pipeline_mode=pl.Buffered(1)),
            pl.BlockSpec((tm, d), lambda i: (i, 0)),
            pl.BlockSpec((1, d), lambda i: (0, 0)),
            pl.BlockSpec((1, d), lambda i: (0, 0)),
        ],
        out_specs=pl.BlockSpec((tm, d), lambda i: (i, 0)),
        out_shape=jax.ShapeDtypeStruct((s, d), F32),
        compiler_params=_cparams(("arbitrary",)),
        name="ffn_down_residual",
    )(act, wd, x, g_post, gate)


def _split2(a):
    hi = a.astype(BF16)
    return hi, (a - hi.astype(F32)).astype(BF16)


def _router_kernel(h_ref, w_ref, b_ref, route_ref, mh_ref):
    hh, hl = _split2(h_ref[...])
    wh, wl = _split2(w_ref[...])
    logits = (jnp.dot(hh, wh, preferred_element_type=F32)
              + jnp.dot(hh, wl, preferred_element_type=F32)
              + jnp.dot(hl, wh, preferred_element_type=F32)) + b_ref[...]
    lane = lax.broadcasted_iota(jnp.int32, logits.shape, 1)
    logits = jnp.where(lane < N_EXPERTS, logits, NEG_INF)
    v1 = jnp.max(logits, axis=-1, keepdims=True)
    i1 = jnp.min(jnp.where(logits == v1, lane, LANES), axis=-1, keepdims=True)
    rest = jnp.where(lane == i1, NEG_INF, logits)
    v2 = jnp.max(rest, axis=-1, keepdims=True)
    i2 = jnp.min(jnp.where(rest == v2, lane, LANES), axis=-1, keepdims=True)
    e = jnp.exp(v2 - v1)
    g1 = 1.0 / (1.0 + e)
    g2 = e / (1.0 + e)
    route_ref[...] = jnp.where(lane == 0, i1.astype(F32),
                               jnp.where(lane == 1, i2.astype(F32),
                                         jnp.where(lane == 2, g1, jnp.where(lane == 3, g2, 0.0))))
    mh_ref[...] = jnp.where(jnp.logical_or(lane == i1, lane == i2), 1.0, 0.0)


def _router(h, rw_pad, rb_pad):
    s, d = h.shape
    tm = _tile(s, 512)
    return pl.pallas_call(
        _router_kernel,
        grid=(s // tm,),
        in_specs=[
            pl.BlockSpec((tm, d), lambda i: (i, 0)),
            pl.BlockSpec((d, LANES), lambda i: (0, 0)),
            pl.BlockSpec((1, LANES), lambda i: (0, 0)),
        ],
        out_specs=[pl.BlockSpec((tm, LANES), lambda i: (i, 0)),
                   pl.BlockSpec((tm, LANES), lambda i: (i, 0))],
        out_shape=[jax.ShapeDtypeStruct((s, LANES), F32), jax.ShapeDtypeStruct((s, LANES), F32)],
        compiler_params=_cparams(("arbitrary",)),
        name="moe_router",
    )(h, rw_pad, rb_pad)


def _dispatch_kernel(pos_ref, h_ref, init_ref, o_ref, sem):
    del init_ref
    n = h_ref.shape[0]

    def copy(r, k):
        return pltpu.make_async_copy(h_ref.at[pl.ds(r, 1)], o_ref.at[pl.ds(pos_ref[0, 0, 2 * r + k], 1)], sem)

    def issue(r, c):
        copy(r, 0).start()
        copy(r, 1).start()
        return c

    def drain(r, c):
        copy(r, 0).wait()
        copy(r, 1).wait()
        return c

    lax.fori_loop(0, n, issue, 0)
    lax.fori_loop(0, n, drain, 0)


def _dispatch(h, pos, n_slots):
    s, d = h.shape
    tm = _tile(s, 256)
    init = jnp.zeros((n_slots, d), F32)
    return pl.pallas_call(
        _dispatch_kernel,
        grid=(s // tm,),
        in_specs=[
            pl.BlockSpec((1, 1, TOP_K * tm), lambda i: (i, 0, 0), memory_space=pltpu.SMEM),
            pl.BlockSpec((tm, d), lambda i: (i, 0)),
            pl.BlockSpec(memory_space=pl.ANY),
        ],
        out_specs=pl.BlockSpec(memory_space=pl.ANY),
        out_shape=jax.ShapeDtypeStruct((n_slots, d), F32),
        scratch_shapes=[pltpu.SemaphoreType.DMA(())],
        input_output_aliases={2: 0},
        compiler_params=_cparams(("arbitrary",)),
        name="moe_dispatch",
    )(pos.reshape(s // tm, 1, TOP_K * tm), h, init)


def _ggateup_kernel(te_ref, nu_ref, h_ref, wg_ref, wu_ref, o_ref, wg_scr, wu_scr):
    i = pl.program_id(1)
    used = i < nu_ref[0]
    new_weights = jnp.logical_or(i == 0, te_ref[i] != te_ref[jnp.maximum(i - 1, 0)])

    @pl.when(jnp.logical_and(used, new_weights))
    def _():
        wg_scr[...] = wg_ref[...].astype(BF16)
        wu_scr[...] = wu_ref[...].astype(BF16)

    @pl.when(used)
    def _():
        h = h_ref[...]
        a = jnp.dot(h, wg_scr[...], preferred_element_type=F32)
        u = jnp.dot(h, wu_scr[...], preferred_element_type=F32)
        o_ref[...] = ((a * jax.nn.sigmoid(a)) * u).astype(BF16)

    @pl.when(jnp.logical_not(used))
    def _():
        o_ref[...] = jnp.zeros_like(o_ref)


def _ggateup(hs, wg, wu, tile_expert, n_used, tm):
    p, d = hs.shape
    f = wg.shape[2]
    tf = _tile(f, 1024)
    return pl.pallas_call(
        _ggateup_kernel,
        grid_spec=pltpu.PrefetchScalarGridSpec(
            num_scalar_prefetch=2,
            grid=(f // tf, p // tm),
            in_specs=[
                pl.BlockSpec((tm, d), lambda j, i, te, nu: (i, 0)),
                pl.BlockSpec((None, d, tf), lambda j, i, te, nu: (te[i], 0, j)),
                pl.BlockSpec((None, d, tf), lambda j, i, te, nu: (te[i], 0, j)),
            ],
            out_specs=pl.BlockSpec((tm, tf), lambda j, i, te, nu: (i, j)),
            scratch_shapes=[pltpu.VMEM((d, tf), BF16), pltpu.VMEM((d, tf), BF16)],
        ),
        out_shape=jax.ShapeDtypeStruct((p, f), BF16),
        compiler_params=_cparams(("arbitrary", "arbitrary")),
        name="moe_gateup",
    )(tile_expert, n_used, hs, wg, wu)


def _gdown_kernel(te_ref, nu_ref, a_ref, w_ref, o_ref, w_scr):
    i = pl.program_id(1)
    used = i < nu_ref[0]
    new_weights = jnp.logical_or(i == 0, te_ref[i] != te_ref[jnp.maximum(i - 1, 0)])

    @pl.when(jnp.logical_and(used, new_weights))
    def _():
        w_scr[...] = w_ref[...].astype(BF16)

    @pl.when(used)
    def _():
        o_ref[...] = jnp.dot(a_ref[...], w_scr[...], preferred_element_type=F32)

    @pl.when(jnp.logical_not(used))
    def _():
        o_ref[...] = jnp.zeros_like(o_ref)


def _gdown(act, wd, tile_expert, n_used, tm):
    p, f = act.shape
    d = wd.shape[2]
    tn = _tile(d, 512)
    return pl.pallas_call(
        _gdown_kernel,
        grid_spec=pltpu.PrefetchScalarGridSpec(
            num_scalar_prefetch=2,
            grid=(d // tn, p // tm),
            in_specs=[
                pl.BlockSpec((tm, f), lambda n, i, te, nu: (i, 0)),
                pl.BlockSpec((None, f, tn), lambda n, i, te, nu: (te[i], 0, n)),
            ],
            out_specs=pl.BlockSpec((tm, tn), lambda n, i, te, nu: (i, n)),
            scratch_shapes=[pltpu.VMEM((f, tn), BF16)],
        ),
        out_shape=jax.ShapeDtypeStruct((p, d), F32),
        compiler_params=_cparams(("arbitrary", "arbitrary")),
        name="moe_down",
    )(tile_expert, n_used, act, wd)


def _combine_kernel(pos_ref, y_ref, route_ref, x_ref, gp_ref, gate_ref, o_ref, buf0, buf1, sem):
    n = x_ref.shape[0]

    def copy(r, k, buf):
        return pltpu.make_async_copy(y_ref.at[pl.ds(pos_ref[0, 0, 2 * r + k], 1)], buf.at[pl.ds(r, 1)], sem)

    def issue(r, c):
        copy(r, 0, buf0).start()
        copy(r, 1, buf1).start()
        return c

    def drain(r, c):
        copy(r, 0, buf0).wait()
        copy(r, 1, buf1).wait()
        return c

    lax.fori_loop(0, n, issue, 0)
    lax.fori_loop(0, n, drain, 0)
    route = route_ref[...]
    y = route[:, 2:3] * buf0[...] + route[:, 3:4] * buf1[...]
    o_ref[...] = _residual(x_ref[...], y, gp_ref[...], gate_ref[...])


def _combine(ys, pos, route, x, g_post, gate):
    s, d = x.shape
    tm = _tile(s, 256)
    return pl.pallas_call(
        _combine_kernel,
        grid=(s // tm,),
        in_specs=[
            pl.BlockSpec((1, 1, TOP_K * tm), lambda i: (i, 0, 0), memory_space=pltpu.SMEM),
            pl.BlockSpec(memory_space=pl.ANY),
            pl.BlockSpec((tm, LANES), lambda i: (i, 0)),
            pl.BlockSpec((tm, d), lambda i: (i, 0)),
            pl.BlockSpec((1, d), lambda i: (0, 0)),
            pl.BlockSpec((1, d), lambda i: (0, 0)),
        ],
        out_specs=pl.BlockSpec((tm, d), lambda i: (i, 0)),
        out_shape=jax.ShapeDtypeStruct((s, d), F32),
        scratch_shapes=[pltpu.VMEM((tm, d), F32), pltpu.VMEM((tm, d), F32), pltpu.SemaphoreType.DMA(())],
        compiler_params=_cparams(("arbitrary",)),
        name="moe_combine_residual",
    )(pos.reshape(s // tm, 1, TOP_K * tm), ys, route, x, g_post, gate)


MOE_TM = 512


def _moe(h, x, router_w, router_b, wg, wu, wd, g_post, gate):
    s, d = h.shape
    pad = lambda a: jnp.pad(a.astype(F32), ((0, 0), (0, LANES - N_EXPERTS)))
    route, mh = _router(h, pad(router_w), pad(router_b.reshape(1, N_EXPERTS)))
    csum, = _cumsum_rows(mh)
    tm = _tile(TOP_K * s, MOE_TM)
    counts = csum[s - 1, :N_EXPERTS].astype(jnp.int32)
    padded = ((counts + tm - 1) // tm) * tm
    ends = jnp.cumsum(padded)
    starts = ends - padded
    ids = route[:, :TOP_K].astype(jnp.int32)
    slot_of = starts[None, :] + (csum[:, :N_EXPERTS] - mh[:, :N_EXPERTS]).astype(jnp.int32)
    pos = jnp.take_along_axis(slot_of, ids, axis=1)
    n_tiles = (TOP_K * s) // tm + N_EXPERTS
    n_used = (ends[N_EXPERTS - 1] // tm).astype(jnp.int32).reshape(1)
    tile_start = jnp.minimum(jnp.arange(n_tiles, dtype=jnp.int32), n_used[0] - 1) * tm
    tile_expert = jnp.sum(tile_start[:, None] >= ends[None, :], axis=1).astype(jnp.int32)
    hs = _dispatch(h, pos, n_tiles * tm).astype(BF16)
    act = _ggateup(hs, wg, wu, tile_expert, n_used, tm)
    ys = _gdown(act, wd, tile_expert, n_used, tm)
    return _combine(ys, pos, route, x, g_post, gate)


def kernel(x, c, positions, mod_w, mod_b, norm_g, w_in, b_f, rel_bias, lam, onorm, w_o,
           ffn_wg, ffn_wu, ffn_wd, router_w, router_b, exp_wg, exp_wu, exp_wd):
    b, s, d = x.shape
    assert b == 1 and s % TQB == 0
    depth = mod_w.shape[0]
    xs = x.reshape(s, d)
    pos_col = positions.reshape(s, 1).astype(F32)
    mods = _mods(c, mod_w, mod_b)

    f0 = 3 * WIDTH_A
    scale_ab = HEAD_DIM ** -0.5 * LOG2E
    scale_c = DIFF_DIM ** -0.5 * LOG2E
    colscale = np.ones((1, w_in.shape[2] - N_HEADS_A), np.float32)
    colscale[0, 0:WIDTH_A] = scale_ab
    colscale[0, f0:f0 + WIDTH_B] = scale_ab
    colscale[0, ROPE_COL_LO:ROPE_COL_LO + WIDTH_C] = scale_c
    colscale = jnp.asarray(colscale)

    for l in range(depth):
        sh_m, sc_m, g_m, sh_f, sc_f, g_f = [mods[l, :, i * d:(i + 1) * d] for i in range(6)]
        ng = lambda i: norm_g[l, i].reshape(1, d).astype(F32)
        w_main = jnp.concatenate([w_in[l, :, :f0], w_in[l, :, f0 + N_HEADS_A:]], axis=1).astype(BF16)
        w_f = jnp.pad(w_in[l, :, f0:f0 + N_HEADS_A], ((0, 0), (0, LANES - N_HEADS_A))).astype(BF16)
        bf = jnp.pad(b_f[l].astype(F32).reshape(1, N_HEADS_A), ((0, 0), (0, LANES - N_HEADS_A)))
        proj, logf = _inproj(xs, ng(0), sc_m, sh_m, w_main, colscale, w_f, bf, pos_col)
        _, decay = _cumsum_rows(logf, decay_heads=N_HEADS_A)
        on = lambda i: onorm[l, i].reshape(1, HEAD_DIM).astype(F32)
        lam_init = 0.8 - 0.6 * math.exp(-0.3 * l)
        oa = _attn_a(proj, decay, on(0))
        ob = _attn_b(proj, _band_bias_table(rel_bias[l]), on(1))
        oc = _attn_c(proj, lam[l].astype(F32), on(2), lam_init)
        moe = l % 2 == 1
        xs, h2 = _outproj(oa, ob, oc, w_o[l].astype(BF16), xs, ng(1), g_m, ng(2), sc_f, sh_f,
                          F32 if moe else BF16)
        i = l // 2
        if moe:
            xs = _moe(h2, xs, router_w[i], router_b[i], exp_wg[i], exp_wu[i], exp_wd[i], ng(3), g_f)
        else:
            act = _gateup(h2, ffn_wg[i].astype(BF16), ffn_wu[i].astype(BF16))
            xs = _down(act, ffn_wd[i].astype(BF16), xs, ng(3), g_f)
    return xs.reshape(b, s, d)
```
